```python
import jax, jax.numpy as jnp
from jax import lax
import numpy as np

D_MODEL = 2048
BATCH = 4
SEQ = 2048
DEPTH = 1
DEC_BATCH = 128
DEC_SEQ = 1
PAST_LEN = 16384
PAGE_SIZE = 128

HG_HEADS = 8
HG_HEAD_DIM = 128
HG_W = HG_HEADS * HG_HEAD_DIM
HG_CHUNK = 64
RW_HEADS = 16
RW_HEAD_DIM = 64
RW_W = RW_HEADS * RW_HEAD_DIM
RW_DECAY_LORA = 64
RW_A_LORA = 64
RW_GATE_LORA = 128
RW_SHIFT_W = 3 * RW_W + RW_DECAY_LORA + RW_A_LORA + RW_GATE_LORA
N_BRANCH = 2
IN_W = 4 * HG_W + RW_SHIFT_W + N_BRANCH * D_MODEL
MEM_TOKENS = 256
CA_HEADS = 4
CA_HEAD_DIM = D_MODEL // CA_HEADS
MOE_GROUPS = 4
MOE_EXPERTS_PER_GROUP = 8
MOE_EXPERTS = MOE_GROUPS * MOE_EXPERTS_PER_GROUP
MOE_HIDDEN = 512
MOE_TOP_K = 2
RMS_EPS = 1e-6
GN_EPS = 64e-5

kernel_name = 'hgrn2_rwkv7_gated_merge_memxattn_hmoe_step'


def rmsnorm(x, g):
    xf = x.astype(jnp.float32)
    y = xf * lax.rsqrt(jnp.mean(xf * xf, axis=-1, keepdims=True) + RMS_EPS)
    return (y * g.astype(jnp.float32)).astype(x.dtype)


def split_cols(p, widths):
    offs = np.cumsum([0] + list(widths)).tolist()
    return [p[..., offs[i]:offs[i + 1]] for i in range(len(widths))]


def to_heads(t, n_heads):
    return t.reshape(t.shape[:-1] + (n_heads, t.shape[-1] // n_heads))


def hgrn2_chunked(q, log_f, k, v, s0):
    B, L, H, DK = q.shape
    DV = v.shape[-1]
    c = min(HG_CHUNK, L)
    n = -(-L // c)
    pad = n * c - L
    if pad:
        padw = ((0, 0), (0, pad), (0, 0), (0, 0))
        q, log_f, k, v = [jnp.pad(t, padw) for t in (q, log_f, k, v)]

    def to_chunks(t):
        return t.reshape(B, n, c, H, t.shape[-1]).transpose(1, 0, 2, 3, 4)

    xs = tuple(to_chunks(t) for t in (q, log_f, k, v))
    causal = jnp.tril(jnp.ones((c, c), dtype=bool))

    def step(s, inp):
        qb, lfb, kb, vb = inp
        b = jnp.cumsum(lfb, axis=1)
        b_last = b[:, -1]
        q_in = qb * jnp.exp(b)
        k_in = kb * jnp.exp(-b)
        o_inter = jnp.einsum('bchk,bhkv->bchv', q_in, s)
        att = jnp.where(causal, jnp.einsum('bchk,bshk->bhcs', q_in, k_in), 0.0)
        o_intra = jnp.einsum('bhcs,bshv->bchv', att, vb)
        k_out = kb * jnp.exp(b_last[:, None] - b)
        s_new = jnp.exp(b_last)[..., None] * s + jnp.einsum('bshk,bshv->bhkv', k_out, vb)
        return s_new, o_inter + o_intra

    s_fin, o = lax.scan(step, s0, xs)
    o = o.transpose(1, 0, 2, 3, 4).reshape(B, n * c, H, DV)[:, :L]
    return o, s_fin


def rwkv7_scan(r, log_w, k, v, kk, a, s0):
    def step(s, inp):
        r_t, lw_t, k_t, v_t, kk_t, a_t = inp
        sa = jnp.einsum('bhvk,bhk->bhv', s, kk_t)
        s = (s * jnp.exp(lw_t)[:, :, None, :]
             - sa[..., None] * (kk_t * a_t)[:, :, None, :]
             + v_t[..., None] * k_t[:, :, None, :])
        y = jnp.einsum('bhvk,bhk->bhv', s, r_t)
        return s, y

    xs = tuple(t.transpose(1, 0, 2, 3) for t in (r, log_w, k, v, kk, a))
    s_fin, y = lax.scan(step, s0, xs)
    return y.transpose(1, 0, 2, 3), s_fin


def token_mixer(h, hg_s0, rw_s0, shift0, lb, P):
    f32 = jnp.float32
    dt = h.dtype
    B, L = h.shape[0], h.shape[1]
    p = jnp.einsum('bld,de->ble', h, P['w_in']).astype(f32)
    hq, hf, hi, hog, p_rw, gate_hg, gate_rw = split_cols(
        p, (HG_W, HG_W, HG_W, HG_W, RW_SHIFT_W, D_MODEL, D_MODEL))

    lb = lb.astype(f32)
    f = lb + (1.0 - lb) * jax.nn.sigmoid(hf)
    o_hg, hg_s = hgrn2_chunked(to_heads(jax.nn.silu(hq), HG_HEADS), to_heads(jnp.log(f), HG_HEADS),
                               to_heads(1.0 - f, HG_HEADS), to_heads(hi, HG_HEADS), hg_s0.astype(f32))
    o_hg = rmsnorm(o_hg, P['hg_norm_g'].reshape(HG_HEADS, HG_HEAD_DIM)).reshape(B, L, HG_W)
    o_hg = o_hg * jax.nn.silu(hog)

    p_prev = jnp.concatenate([shift0.astype(f32)[:, None, :], p_rw[:, :-1]], axis=1)
    p_mix = p_rw + P['rw_mu'] * (p_prev - p_rw)
    new_shift = p_rw[:, -1]
    rr, rk, rv, rwl, ral, rgl = split_cols(
        p_mix, (RW_W, RW_W, RW_W, RW_DECAY_LORA, RW_A_LORA, RW_GATE_LORA))
    w_raw = -jax.nn.softplus(-(P['rw_w0'] + jnp.tanh(rwl) @ P['rw_w2'])) - 0.5
    log_w = -jnp.exp(w_raw)
    a = jax.nn.sigmoid(P['rw_a0'] + ral @ P['rw_a2'])
    g = jax.nn.sigmoid(rgl) @ P['rw_g2']
    kk = to_heads(rk * P['rw_kk'], RW_HEADS)
    kk = kk / jnp.maximum(jnp.sqrt(jnp.sum(kk * kk, axis=-1, keepdims=True)), 1e-12)
    k = rk * (1.0 + (a - 1.0) * P['rw_ka'])
    r_h, k_h, v_h = to_heads(rr, RW_HEADS), to_heads(k, RW_HEADS), to_heads(rv, RW_HEADS)
    o_rw, rw_s = rwkv7_scan(r_h, to_heads(log_w, RW_HEADS), k_h, v_h, kk,
                            to_heads(a, RW_HEADS), rw_s0.astype(f32))
    mu = jnp.mean(o_rw, axis=-1, keepdims=True)
    var = jnp.mean(jnp.square(o_rw - mu), axis=-1, keepdims=True)
    o_rw = ((o_rw - mu) * lax.rsqrt(var + GN_EPS)).reshape(B, L, RW_W) * P['rw_ln_g'] + P['rw_ln_b']
    bonus = jnp.sum(r_h * k_h * P['rw_rk'].reshape(RW_HEADS, RW_HEAD_DIM), axis=-1, keepdims=True) * v_h
    o_rw = (o_rw + bonus.reshape(B, L, RW_W)) * g

    merged = (jax.nn.sigmoid(gate_hg) * (o_hg @ P['w_br_hg'])
              + jax.nn.sigmoid(gate_rw) * (o_rw @ P['w_br_rw']))
    out = jnp.einsum('bld,de->ble', merged.astype(dt), P['w_out']).astype(dt)
    return out, hg_s.astype(hg_s0.dtype), rw_s.astype(rw_s0.dtype), new_shift.astype(shift0.dtype)


def mem_kv(mem, g_mem, w_k, w_v):
    m = rmsnorm(mem, g_mem)
    k = jnp.einsum('bmd,de->bme', m, w_k)
    v = jnp.einsum('bmd,de->bme', m, w_v)
    return to_heads(k, CA_HEADS), to_heads(v, CA_HEADS)


def cross_attn(h, mem_k, mem_v, w_q, w_o):
    f32 = jnp.float32
    q = to_heads(jnp.einsum('bld,de->ble', h, w_q), CA_HEADS)
    s = jnp.einsum('blhd,bmhd->bhlm', q.astype(f32), mem_k.astype(f32)) * (CA_HEAD_DIM ** -0.5)
    p = jax.nn.softmax(s, axis=-1)
    o = jnp.einsum('bhlm,bmhd->blhd', p, mem_v.astype(f32)).astype(h.dtype)
    o = o.reshape(h.shape[0], h.shape[1], D_MODEL)
    return jnp.einsum('bld,de->ble', o, w_o)


def hier_moe(h, P):
    f32 = jnp.float32
    lead = h.shape[:-1]
    t = h.reshape(-1, D_MODEL)
    g_logits = (t @ P['w_rg']).astype(f32) + P['b_rg']
    g_prob = jax.nn.softmax(g_logits, axis=-1)
    g_top = jnp.argmax(g_logits, axis=-1)
    g_w = jnp.take_along_axis(g_prob, g_top[:, None], axis=-1)
    e_logits = ((t @ P['w_re']).astype(f32) + P['b_re']).reshape(-1, MOE_GROUPS, MOE_EXPERTS_PER_GROUP)
    e_sel = jnp.take_along_axis(e_logits, g_top[:, None, None], axis=1)[:, 0]
    e_prob = jax.nn.softmax(e_sel, axis=-1)
    top_w, top_i = lax.top_k(e_prob, MOE_TOP_K)
    top_w = top_w / jnp.sum(top_w, axis=-1, keepdims=True)
    expert_id = g_top[:, None] * MOE_EXPERTS_PER_GROUP + top_i
    combine = jnp.sum(jax.nn.one_hot(expert_id, MOE_EXPERTS, dtype=f32)
                      * (g_w * top_w)[..., None], axis=1)
    hid = jax.nn.silu(jnp.einsum('td,edf->tef', t, P['w_e1'])) * jnp.einsum('td,edf->tef', t, P['w_e3'])
    y = jnp.einsum('tef,efd->td', hid * combine[..., None].astype(hid.dtype), P['w_e2'])
    return y.reshape(lead + (D_MODEL,)).astype(h.dtype)


def decoder_layer(x, mem_k, mem_v, hg_s0, rw_s0, shift0, lb, P):
    mix, hg_s, rw_s, shift = token_mixer(rmsnorm(x, P['g_mix']), hg_s0, rw_s0, shift0, lb, P)
    x = x + mix
    x = x + cross_attn(rmsnorm(x, P['g_ca']), mem_k, mem_v, P['w_ca_q'], P['w_ca_o'])
    x = x + hier_moe(rmsnorm(x, P['g_moe']), P)
    return x, hg_s, rw_s, shift


def setup_inputs(seed: int = 0) -> dict:
    key = jax.random.key(seed)
    ks = iter(jax.random.split(key, 64))
    f32 = jnp.float32

    def nrm(shape, scale=1.0):
        return jax.random.normal(next(ks), shape, f32) * scale

    def gain(shape):
        return 1.0 + nrm(shape, 0.02)

    def unif(shape, lo, hi):
        return jax.random.uniform(next(ks), shape, f32, lo, hi)

    return {
        'x_prompt': nrm((BATCH, SEQ, D_MODEL)),
        'x_sample': nrm((DEC_BATCH, DEC_SEQ, D_MODEL)),
        'mem_prompt': nrm((BATCH, MEM_TOKENS, D_MODEL)),
        'cache_mem_k': nrm((DEPTH, DEC_BATCH, MEM_TOKENS, CA_HEADS, CA_HEAD_DIM)),
        'cache_mem_v': nrm((DEPTH, DEC_BATCH, MEM_TOKENS, CA_HEADS, CA_HEAD_DIM)),
        'state_hgrn': nrm((DEPTH, DEC_BATCH, HG_HEADS, HG_HEAD_DIM, HG_HEAD_DIM)),
        'state_rwkv': nrm((DEPTH, DEC_BATCH, RW_HEADS, RW_HEAD_DIM, RW_HEAD_DIM)),
        'state_rwkv_shift': nrm((DEPTH, DEC_BATCH, RW_SHIFT_W)),
        'g_mix': gain((DEPTH, D_MODEL)),
        'w_in': nrm((DEPTH, D_MODEL, IN_W), D_MODEL ** -0.5),
        'hg_lb_logits': nrm((DEPTH + 1, HG_W), 0.1),
        'hg_norm_g': gain((DEPTH, HG_W)),
        'rw_mu': unif((DEPTH, RW_SHIFT_W), 0.0, 1.0),
        'rw_w0': unif((DEPTH, RW_W), -6.5, -1.5),
        'rw_w2': nrm((DEPTH, RW_DECAY_LORA, RW_W), 0.1 * RW_DECAY_LORA ** -0.5),
        'rw_a0': nrm((DEPTH, RW_W), 0.1),
        'rw_a2': nrm((DEPTH, RW_A_LORA, RW_W), 0.1 * RW_A_LORA ** -0.5),
        'rw_g2': nrm((DEPTH, RW_GATE_LORA, RW_W), RW_GATE_LORA ** -0.5),
        'rw_kk': 0.85 + nrm((DEPTH, RW_W), 0.02),
        'rw_ka': gain((DEPTH, RW_W)),
        'rw_rk': nrm((DEPTH, RW_W), 0.1),
        'rw_ln_g': gain((DEPTH, RW_W)),
        'rw_ln_b': nrm((DEPTH, RW_W), 0.01),
        'w_br_hg': nrm((DEPTH, HG_W, D_MODEL), HG_W ** -0.5),
        'w_br_rw': nrm((DEPTH, RW_W, D_MODEL), RW_W ** -0.5),
        'w_out': nrm((DEPTH, D_MODEL, D_MODEL), D_MODEL ** -0.5),
        'g_ca': gain((DEPTH, D_MODEL)),
        'g_mem': gain((DEPTH, D_MODEL)),
        'w_ca_q': nrm((DEPTH, D_MODEL, D_MODEL), D_MODEL ** -0.5),
        'w_ca_k': nrm((DEPTH, D_MODEL, D_MODEL), D_MODEL ** -0.5),
        'w_ca_v': nrm((DEPTH, D_MODEL, D_MODEL), D_MODEL ** -0.5),
        'w_ca_o': nrm((DEPTH, D_MODEL, D_MODEL), D_MODEL ** -0.5),
        'g_moe': gain((DEPTH, D_MODEL)),
        'w_rg': nrm((DEPTH, D_MODEL, MOE_GROUPS), D_MODEL ** -0.5),
        'b_rg': nrm((DEPTH, MOE_GROUPS), 0.01),
        'w_re': nrm((DEPTH, D_MODEL, MOE_EXPERTS), D_MODEL ** -0.5),
        'b_re': nrm((DEPTH, MOE_EXPERTS), 0.01),
        'w_e1': nrm((DEPTH, MOE_EXPERTS, D_MODEL, MOE_HIDDEN), D_MODEL ** -0.5),
        'w_e3': nrm((DEPTH, MOE_EXPERTS, D_MODEL, MOE_HIDDEN), D_MODEL ** -0.5),
        'w_e2': nrm((DEPTH, MOE_EXPERTS, MOE_HIDDEN, D_MODEL), MOE_HIDDEN ** -0.5),
        'g_final': gain((D_MODEL,)),
    }


def reference(x_prompt, x_sample, mem_prompt, cache_mem_k, cache_mem_v, state_hgrn, state_rwkv,
              state_rwkv_shift, g_mix, w_in, hg_lb_logits, hg_norm_g, rw_mu, rw_w0, rw_w2, rw_a0,
              rw_a2, rw_g2, rw_kk, rw_ka, rw_rk, rw_ln_g, rw_ln_b, w_br_hg, w_br_rw, w_out, g_ca,
              g_mem, w_ca_q, w_ca_k, w_ca_v, w_ca_o, g_moe, w_rg, b_rg, w_re, b_re, w_e1, w_e3,
              w_e2, g_final):
    lb_all = jnp.cumsum(jax.nn.softmax(hg_lb_logits.astype(jnp.float32), axis=0), axis=0)
    bp = x_prompt.shape[0]
    xp, xs = x_prompt, x_sample
    hg_p_l, rw_p_l, sh_p_l, mk_p_l, mv_p_l = [], [], [], [], []
    hg_s_l, rw_s_l, sh_s_l = [], [], []
    for l in range(DEPTH):
        P = {'g_mix': g_mix[l], 'w_in': w_in[l], 'hg_norm_g': hg_norm_g[l], 'rw_mu': rw_mu[l],
             'rw_w0': rw_w0[l], 'rw_w2': rw_w2[l], 'rw_a0': rw_a0[l], 'rw_a2': rw_a2[l],
             'rw_g2': rw_g2[l], 'rw_kk': rw_kk[l], 'rw_ka': rw_ka[l], 'rw_rk': rw_rk[l],
             'rw_ln_g': rw_ln_g[l], 'rw_ln_b': rw_ln_b[l], 'w_br_hg': w_br_hg[l],
             'w_br_rw': w_br_rw[l], 'w_out': w_out[l], 'g_ca': g_ca[l], 'w_ca_q': w_ca_q[l],
             'w_ca_o': w_ca_o[l], 'g_moe': g_moe[l], 'w_rg': w_rg[l], 'b_rg': b_rg[l],
             'w_re': w_re[l], 'b_re': b_re[l], 'w_e1': w_e1[l], 'w_e3': w_e3[l], 'w_e2': w_e2[l]}
        lb = lb_all[l]
        mk_p, mv_p = mem_kv(mem_prompt, g_mem[l], w_ca_k[l], w_ca_v[l])
        hg0 = jnp.zeros((bp, HG_HEADS, HG_HEAD_DIM, HG_HEAD_DIM), x_prompt.dtype)
        rw0 = jnp.zeros((bp, RW_HEADS, RW_HEAD_DIM, RW_HEAD_DIM), x_prompt.dtype)
        sh0 = jnp.zeros((bp, RW_SHIFT_W), x_prompt.dtype)
        xp, hg_p, rw_p, sh_p = decoder_layer(xp, mk_p, mv_p, hg0, rw0, sh0, lb, P)
        xs, hg_s, rw_s, sh_s = decoder_layer(xs, cache_mem_k[l], cache_mem_v[l], state_hgrn[l],
                                             state_rwkv[l], state_rwkv_shift[l], lb, P)
        hg_p_l.append(hg_p); rw_p_l.append(rw_p); sh_p_l.append(sh_p)
        mk_p_l.append(mk_p.astype(cache_mem_k.dtype)); mv_p_l.append(mv_p.astype(cache_mem_v.dtype))
        hg_s_l.append(hg_s); rw_s_l.append(rw_s); sh_s_l.append(sh_s)
    y_prompt = rmsnorm(xp, g_final)
    y_sample = rmsnorm(xs, g_final)
    new_hgrn_prompt = jnp.stack(hg_p_l)
    new_rwkv_prompt = jnp.stack(rw_p_l)
    new_shift_prompt = jnp.stack(sh_p_l)
    new_mem_k_prompt = jnp.stack(mk_p_l)
    new_mem_v_prompt = jnp.stack(mv_p_l)
    new_hgrn_sample = jnp.stack(hg_s_l)
    new_rwkv_sample = jnp.stack(rw_s_l)
    new_shift_sample = jnp.stack(sh_s_l)
    return (y_prompt, y_sample, new_hgrn_prompt, new_rwkv_prompt, new_shift_prompt,
            new_mem_k_prompt, new_mem_v_prompt, new_hgrn_sample, new_rwkv_sample, new_shift_sample)
```

```python
import functools

import jax
import jax.numpy as jnp
from jax import lax
from jax.experimental import pallas as pl
from jax.experimental.pallas import tpu as pltpu

F32 = jnp.float32
BF16 = jnp.bfloat16

RMS_EPS = 1e-6
GN_EPS = 64e-5

LANES = 128
VMEM_LIMIT_BYTES = 56 * 1024 * 1024

HG_HEADS = 8
HG_DIM = 128
HG_CHUNK = 64
RW_HEADS = 16
RW_DIM = 64
RW_CHUNK = 64
RW_SUB = 16
CA_HEADS = 4
MOE_GROUPS = 4
MOE_PER_GROUP = 8
MOE_EXPERTS = MOE_GROUPS * MOE_PER_GROUP
MOE_TILE = 256


def _params(*sem, vmem=None):
    return pltpu.CompilerParams(dimension_semantics=sem, vmem_limit_bytes=vmem)


def _nt(a, b):
    return lax.dot_general(a, b, (((1,), (1,)), ((), ())), preferred_element_type=F32)


def _tn(a, b):
    return lax.dot_general(a, b, (((0,), (0,)), ((), ())), preferred_element_type=F32)


def _nn(a, b):
    return jnp.dot(a, b, preferred_element_type=F32)


def _split2(x):
    hi = x.astype(BF16)
    lo = (x - hi.astype(F32)).astype(BF16)
    return hi, lo


def _split3(x):
    hi = x.astype(BF16)
    r = x - hi.astype(F32)
    mid = r.astype(BF16)
    lo = (r - mid.astype(F32)).astype(BF16)
    return hi, mid, lo


def _dot3(a, b, dot=_nn):
    ah, al = _split2(a)
    bh, bl = _split2(b)
    return dot(ah, bh) + (dot(ah, bl) + dot(al, bh))


def _dot_exact_rhs(a, b_bf16, dot=_nn):
    h, m, l = _split3(a)
    return dot(h, b_bf16) + (dot(m, b_bf16) + dot(l, b_bf16))


def _iota2(shape, dim):
    return lax.broadcasted_iota(jnp.int32, shape, dim)


def _silu(x):
    return x * jax.nn.sigmoid(x)


def _rms_kernel(x_ref, g_ref, o_ref):
    x = x_ref[...]
    ms = jnp.mean(x * x, axis=-1, keepdims=True)
    o_ref[...] = (x * lax.rsqrt(ms + RMS_EPS) * g_ref[...]).astype(o_ref.dtype)


def _rmsnorm(x, g, out_dtype, tm):
    t, d = x.shape
    return pl.pallas_call(
        _rms_kernel,
        out_shape=jax.ShapeDtypeStruct((t, d), out_dtype),
        grid=(t // tm,),
        in_specs=[pl.BlockSpec((tm, d), lambda i: (i, 0)), pl.BlockSpec((1, d), lambda i: (0, 0))],
        out_specs=pl.BlockSpec((tm, d), lambda i: (i, 0)),
        compiler_params=_params("parallel"),
        name="rmsnorm",
    )(x, g.reshape(1, d))


def _mm_kernel(x_ref, w_ref, *rest, has_res):
    if has_res:
        r_ref, o_ref, wb_ref = rest
    else:
        o_ref, wb_ref = rest

    @pl.when(pl.program_id(1) == 0)
    def _():
        wb_ref[...] = w_ref[...].astype(BF16)

    acc = _nn(x_ref[...], wb_ref[...])
    if has_res:
        acc = r_ref[...] + acc
    o_ref[...] = acc.astype(o_ref.dtype)


def _matmul(x, w, *, tm, tn, out_dtype, res=None):
    m, k = x.shape
    n = w.shape[1]
    in_specs = [pl.BlockSpec((tm, k), lambda j, i: (i, 0)), pl.BlockSpec((k, tn), lambda j, i: (0, j))]
    args = [x, w]
    if res is not None:
        in_specs.append(pl.BlockSpec((tm, tn), lambda j, i: (i, j)))
        args.append(res)
    return pl.pallas_call(
        functools.partial(_mm_kernel, has_res=res is not None),
        out_shape=jax.ShapeDtypeStruct((m, n), out_dtype),
        grid=(n // tn, m // tm),
        in_specs=in_specs,
        out_specs=pl.BlockSpec((tm, tn), lambda j, i: (i, j)),
        scratch_shapes=[pltpu.VMEM((k, tn), BF16)],
        compiler_params=_params("parallel", "arbitrary", vmem=VMEM_LIMIT_BYTES),
        name="matmul",
    )(*args)


def _hgrn_gates(hq, hf, lb):
    q = _silu(hq)
    f = lb + (1.0 - lb) * jax.nn.sigmoid(hf)
    return q, f


def _hgrn_out(o, gn, hog):
    ms = jnp.mean(o * o, axis=-1, keepdims=True)
    return o * lax.rsqrt(ms + RMS_EPS) * gn * _silu(hog)


def _hgrn_prompt_kernel(q_ref, f_ref, i_ref, og_ref, lb_ref, gn_ref, o_ref, s_ref, st_ref, *, n_chunks):
    l = pl.program_id(2)
    c = HG_CHUNK

    @pl.when(l == 0)
    def _():
        st_ref[...] = jnp.zeros_like(st_ref)

    lb = lb_ref[...]
    gn = gn_ref[...]
    incl = _iota2((c, c), 0) >= _iota2((c, c), 1)
    tri = incl.astype(BF16)
    for ci in range(n_chunks):
        sl = slice(ci * c, (ci + 1) * c)
        q, f = _hgrn_gates(q_ref[sl, :], f_ref[sl, :], lb)
        k = 1.0 - f
        vb = i_ref[sl, :].astype(BF16)
        b = _cumsum_rows(jnp.log(f), tri)
        b_last = b[c - 1:c, :]
        q_in = (q * jnp.exp(b)).astype(BF16)
        k_in = (k * jnp.exp(-b)).astype(BF16)
        k_out = (k * jnp.exp(b_last - b)).astype(BF16)
        st = st_ref[...]
        o_inter = _nt(q_in, st.astype(BF16))
        att = jnp.where(incl, _nt(q_in, k_in), 0.0)
        o_intra = _nn(att.astype(BF16), vb)
        st_ref[...] = st * jnp.exp(b_last) + _tn(vb, k_out)
        o_ref[sl, :] = _hgrn_out(o_inter + o_intra, gn, og_ref[sl, :]).astype(o_ref.dtype)

    @pl.when(l == pl.num_programs(2) - 1)
    def _():
        s_ref[0, 0] = st_ref[...].T


def _cumsum_rows(x, tri_bf16):
    h, m, l = _split3(x)
    return _nn(tri_bf16, h) + (_nn(tri_bf16, m) + _nn(tri_bf16, l))


def _hgrn_prompt(p_all, lb, gn, *, batch, seq, col0, lblk):
    nl = seq // lblk
    hw = HG_HEADS * HG_DIM
    nb = hw // LANES

    def pspec(seg):
        return pl.BlockSpec((lblk, LANES), lambda b, h, l: (b * nl + l, col0 + seg * nb + h))

    vec = pl.BlockSpec((1, LANES), lambda b, h, l: (0, h))
    return pl.pallas_call(
        functools.partial(_hgrn_prompt_kernel, n_chunks=lblk // HG_CHUNK),
        out_shape=(jax.ShapeDtypeStruct((batch * seq, hw), BF16),
                   jax.ShapeDtypeStruct((batch, HG_HEADS, HG_DIM, HG_DIM), F32)),
        grid=(batch, HG_HEADS, nl),
        in_specs=[pspec(0), pspec(1), pspec(2), pspec(3), vec, vec],
        out_specs=(pl.BlockSpec((lblk, LANES), lambda b, h, l: (b * nl + l, h)),
                   pl.BlockSpec((1, 1, HG_DIM, HG_DIM), lambda b, h, l: (b, h, 0, 0))),
        scratch_shapes=[pltpu.VMEM((HG_DIM, HG_DIM), F32)],
        compiler_params=_params("parallel", "parallel", "arbitrary"),
        name="hgrn_prompt",
    )(p_all, p_all, p_all, p_all, lb.reshape(1, hw), gn.reshape(1, hw))


_RW_W0, _RW_A0, _RW_KK, _RW_KA, _RW_RK, _RW_LNG, _RW_LNB = range(7)


def _head_ones():
    same = (_iota2((LANES, LANES), 0) // RW_DIM) == (_iota2((LANES, LANES), 1) // RW_DIM)
    return same.astype(BF16)


def _head_sum(x, ones_bd):
    return _dot_exact_rhs(x, ones_bd)


def _rw_prep(xr, xk, xv, xc, vec, lora, ones_bd):
    c1 = xc[:, :LANES]
    w_lora = _nn(jnp.tanh(c1).astype(BF16), lora[0].astype(BF16))
    a_lora = _nn(c1.astype(BF16), lora[1].astype(BF16))
    g = _nn(jax.nn.sigmoid(xc[:, LANES:]).astype(BF16), lora[2].astype(BF16))
    z = -(vec[_RW_W0:_RW_W0 + 1] + w_lora)
    softplus = jnp.maximum(z, 0.0) + jnp.log1p(jnp.exp(-jnp.abs(z)))
    lw = -jnp.exp(-softplus - 0.5)
    a = jax.nn.sigmoid(vec[_RW_A0:_RW_A0 + 1] + a_lora)
    kk = xk * vec[_RW_KK:_RW_KK + 1]
    kk = kk / jnp.maximum(jnp.sqrt(_head_sum(kk * kk, ones_bd)), 1e-12)
    k = xk * (1.0 + (a - 1.0) * vec[_RW_KA:_RW_KA + 1])
    bonus = _head_sum(xr * k * vec[_RW_RK:_RW_RK + 1], ones_bd) * xv
    return xr, k, xv, kk, kk * a, lw, g, bonus


def _rw_post(y, bonus, g, vec, ones_bd):
    inv = 1.0 / RW_DIM
    mu = _head_sum(y, ones_bd) * inv
    d = y - mu
    var = _head_sum(d * d, ones_bd) * inv
    yn = d * lax.rsqrt(var + GN_EPS) * vec[_RW_LNG:_RW_LNG + 1] + vec[_RW_LNB:_RW_LNB + 1]
    return (yn + bonus) * g


def _unit_lower_inverse(lm):
    c = lm.shape[0]
    row, col = _iota2((c, c), 0), _iota2((c, c), 1)
    eye = (row == col).astype(F32)
    diag_blk = (row // RW_SUB) == (col // RW_SUB)
    n = jnp.where(diag_blk, -lm, 0.0)
    e = jnp.where(diag_blk, 0.0, lm)
    t_d = eye + n
    pw = n
    for _ in range(RW_SUB.bit_length() - 2):
        pw = _dot3(pw, pw)
        t_d = t_d + _dot3(t_d, pw)
    m = _dot3(t_d, e)
    nblk = c // RW_SUB
    acc = eye - m
    pw = m
    for _ in range(nblk.bit_length() - 2):
        pw = _dot3(pw, pw)
        acc = acc + _dot3(acc, pw)
    return _dot3(acc, t_d)


def _rw_chunk(r, k, v, kk, kka, lw, a_ref):
    c = r.shape[0]
    row, col = _iota2((c, c), 0), _iota2((c, c), 1)
    incl, strict = row >= col, row > col
    ginc = _cumsum_rows(lw, incl.astype(BF16))
    glast = ginc[c - 1:c, :]
    e_neg = jnp.exp(-ginc)
    e_rem = jnp.exp(glast - ginc)
    a_hat = kk * jnp.exp(ginc - lw)
    r_hat = r * jnp.exp(ginc)
    e_last = jnp.exp(glast)
    eye = _iota2((RW_DIM, RW_DIM), 0) == _iota2((RW_DIM, RW_DIM), 1)
    ys = []
    for h in range(2):
        s = slice(h * RW_DIM, (h + 1) * RW_DIM)
        ah, rh = a_hat[:, s], r_hat[:, s]
        ab, rb = ah.astype(BF16), rh.astype(BF16)
        bb, kb = (kka * e_neg)[:, s].astype(BF16), (k * e_neg)[:, s].astype(BF16)
        bt, kt = (kka * e_rem)[:, s].astype(BF16), (k * e_rem)[:, s].astype(BF16)
        vb = v[:, s].astype(BF16)
        lm = jnp.where(strict, _nt(ab, bb), 0.0)
        lk = jnp.where(strict, _nt(ab, kb), 0.0)
        pb = jnp.where(incl, _nt(rb, bb), 0.0).astype(BF16)
        pk = jnp.where(incl, _nt(rb, kb), 0.0).astype(BF16)
        t = _unit_lower_inverse(lm)
        x = jnp.concatenate([ah, _nn(lk.astype(BF16), vb)], axis=1)
        tx = _dot3(t, x)
        txb = tx.astype(BF16)
        p_au = _nn(pb, txb)
        b_au = _tn(bt, txb)
        r_til = rh - p_au[:, :RW_DIM]
        y_loc = _nn(pk, vb) - p_au[:, RW_DIM:]
        g_mat = jnp.where(eye, e_last[:, s], 0.0) - b_au[:, :RW_DIM]
        a_loc = _tn(kt, vb) - b_au[:, RW_DIM:]
        a0 = a_ref[h]
        ys.append(_dot3(r_til, a0) + y_loc)
        a_ref[h] = _dot3(g_mat, a0) + a_loc
    return jnp.concatenate(ys, axis=1)


def _rwkv_prompt_kernel(pr_ref, pk_ref, pv_ref, pc_ref, sr_ref, sk_ref, sv_ref, sc_ref,
                        mr_ref, mk_ref, mv_ref, mc_ref, vec_ref, lora_ref,
                        o_ref, s_ref, a_ref, y_ref, cr_ref, ck_ref, cv_ref, cc_ref, *, n_chunks):
    l = pl.program_id(2)
    lblk = pr_ref.shape[0]
    c = RW_CHUNK

    @pl.when(l == 0)
    def _():
        a_ref[...] = jnp.zeros_like(a_ref)
        cr_ref[...] = sr_ref[0]
        ck_ref[...] = sk_ref[0]
        cv_ref[...] = sv_ref[0]
        cc_ref[...] = sc_ref[0]

    def shift_mix(x_ref, carry_ref, mu_ref):
        x = x_ref[...]
        first = _iota2(x.shape, 0) == 0
        prev = jnp.where(first, carry_ref[...], pltpu.roll(x, 1, 0))
        carry_ref[...] = x[lblk - 1:lblk, :]
        return x + mu_ref[...] * (prev - x)

    ones_bd = _head_ones()
    vec = vec_ref[...]
    r, k, v, kk, kka, lw, g, bonus = _rw_prep(
        shift_mix(pr_ref, cr_ref, mr_ref), shift_mix(pk_ref, ck_ref, mk_ref), shift_mix(pv_ref, cv_ref, mv_ref),
        shift_mix(pc_ref, cc_ref, mc_ref), vec, lora_ref[...], ones_bd)
    for ci in range(n_chunks):
        sl = slice(ci * c, (ci + 1) * c)
        y_ref[sl, :] = _rw_chunk(r[sl], k[sl], v[sl], kk[sl], kka[sl], lw[sl], a_ref)
    o_ref[...] = _rw_post(y_ref[...], bonus, g, vec, ones_bd).astype(o_ref.dtype)

    @pl.when(l == pl.num_programs(2) - 1)
    def _():
        for h in range(2):
            s_ref[0, h] = a_ref[h].T


def _rw_param_blocks(P):
    rw_w = RW_HEADS * RW_DIM
    vec = jnp.stack([P['rw_w0'], P['rw_a0'], P['rw_kk'], P['rw_ka'], P['rw_rk'], P['rw_ln_g'], P['rw_ln_b'],
                     jnp.zeros((rw_w,), F32)])
    z = jnp.zeros_like(P['rw_w2'])
    lora = jnp.stack([jnp.concatenate([P['rw_w2'], z]), jnp.concatenate([z, P['rw_a2']]), P['rw_g2']])
    return vec, lora


def _rwkv_prompt(p_all, shift0, mu, vec, lora, *, batch, seq, col0, lblk):
    nl = seq // lblk
    rw_w = RW_HEADS * RW_DIM
    nb = rw_w // LANES
    code_blk = (col0 + 3 * nb) // 2
    shift_w = shift0.shape[-1]

    def pspec(seg):
        return pl.BlockSpec((lblk, LANES), lambda b, h, l: (b * nl + l, col0 + seg * nb + h))

    def sspec(seg):
        return pl.BlockSpec((1, 1, LANES), lambda b, h, l: (b, 0, seg * nb + h))

    def mspec(seg):
        return pl.BlockSpec((1, LANES), lambda b, h, l: (0, seg * nb + h))

    in_specs = [pspec(0), pspec(1), pspec(2),
                pl.BlockSpec((lblk, 2 * LANES), lambda b, h, l: (b * nl + l, code_blk)),
                sspec(0), sspec(1), sspec(2),
                pl.BlockSpec((1, 1, 2 * LANES), lambda b, h, l: (b, 0, 3 * nb // 2)),
                mspec(0), mspec(1), mspec(2),
                pl.BlockSpec((1, 2 * LANES), lambda b, h, l: (0, 3 * nb // 2)),
                pl.BlockSpec((8, LANES), lambda b, h, l: (0, h)),
                pl.BlockSpec((3, LANES, LANES), lambda b, h, l: (0, 0, h))]
    sh3 = shift0.reshape(batch, 1, shift_w)
    mu2 = mu.reshape(1, shift_w)
    return pl.pallas_call(
        functools.partial(_rwkv_prompt_kernel, n_chunks=lblk // RW_CHUNK),
        out_shape=(jax.ShapeDtypeStruct((batch * seq, rw_w), BF16),
                   jax.ShapeDtypeStruct((batch, RW_HEADS, RW_DIM, RW_DIM), F32)),
        grid=(batch, nb, nl),
        in_specs=in_specs,
        out_specs=(pl.BlockSpec((lblk, LANES), lambda b, h, l: (b * nl + l, h)),
                   pl.BlockSpec((1, 2, RW_DIM, RW_DIM), lambda b, h, l: (b, h, 0, 0))),
        scratch_shapes=[pltpu.VMEM((2, RW_DIM, RW_DIM), F32), pltpu.VMEM((lblk, LANES), F32),
                        pltpu.VMEM((1, LANES), F32), pltpu.VMEM((1, LANES), F32), pltpu.VMEM((1, LANES), F32),
                        pltpu.VMEM((1, 2 * LANES), F32)],
        compiler_params=_params("parallel", "parallel", "arbitrary"),
        name="rwkv_prompt",
    )(p_all, p_all, p_all, p_all, sh3, sh3, sh3, sh3, mu2, mu2, mu2, mu2, vec, lora)


STEP_REQ = 16


def _transpose_rows(x):
    r = x.shape[0]
    eye = (_iota2((r, r), 0) == _iota2((r, r), 1)).astype(BF16)
    return _dot_exact_rhs(x, eye, dot=_tn)


def _hgrn_step_kernel(q_ref, f_ref, i_ref, og_ref, lb_ref, gn_ref, s_ref, o_ref, so_ref):
    q, f = _hgrn_gates(q_ref[...], f_ref[...], lb_ref[...])
    f_t = _transpose_rows(f)
    q_t = _transpose_rows(q)
    v = i_ref[...]
    rows = []
    for j in range(STEP_REQ):
        fc = f_t[:, j:j + 1]
        s_new = s_ref[j, 0] * fc + (1.0 - fc) * v[j:j + 1, :]
        so_ref[j, 0] = s_new
        rows.append(jnp.sum(s_new * q_t[:, j:j + 1], axis=0, keepdims=True))
    o = jnp.concatenate(rows, axis=0)
    o_ref[...] = _hgrn_out(o, gn_ref[...], og_ref[...]).astype(o_ref.dtype)


def _hgrn_step(p_all, state, lb, gn, *, row0, col0):
    n = state.shape[0]
    hw = HG_HEADS * HG_DIM
    nb = hw // LANES
    rb0 = row0 // STEP_REQ

    def pspec(seg):
        return pl.BlockSpec((STEP_REQ, LANES), lambda i, h: (rb0 + i, col0 + seg * nb + h))

    vec = pl.BlockSpec((1, LANES), lambda i, h: (0, h))
    sspec = pl.BlockSpec((STEP_REQ, 1, HG_DIM, HG_DIM), lambda i, h: (i, h, 0, 0))
    return pl.pallas_call(
        _hgrn_step_kernel,
        out_shape=(jax.ShapeDtypeStruct((n, hw), BF16), jax.ShapeDtypeStruct(state.shape, F32)),
        grid=(n // STEP_REQ, HG_HEADS),
        in_specs=[pspec(0), pspec(1), pspec(2), pspec(3), vec, vec, sspec],
        out_specs=(pl.BlockSpec((STEP_REQ, LANES), lambda i, h: (i, h)), sspec),
        compiler_params=_params("parallel", "parallel"),
        name="hgrn_step",
    )(p_all, p_all, p_all, p_all, lb.reshape(1, hw), gn.reshape(1, hw), state)


def _rwkv_step_prep_kernel(pr_ref, pk_ref, pv_ref, pc_ref, sr_ref, sk_ref, sv_ref, sc_ref,
                           mr_ref, mk_ref, mv_ref, mc_ref, vec_ref, lora_ref,
                           r_ref, k_ref, v_ref, kk_ref, kka_ref, w_ref, g_ref, bonus_ref):
    def mix(x_ref, prev_ref, mu_ref):
        x = x_ref[...]
        return x + mu_ref[...] * (prev_ref[...] - x)

    r, k, v, kk, kka, lw, g, bonus = _rw_prep(
        mix(pr_ref, sr_ref, mr_ref), mix(pk_ref, sk_ref, mk_ref), mix(pv_ref, sv_ref, mv_ref),
        mix(pc_ref, sc_ref, mc_ref), vec_ref[...], lora_ref[...], _head_ones())
    r_ref[...] = r
    k_ref[...] = k
    v_ref[...] = v
    kk_ref[...] = kk
    kka_ref[...] = kka
    w_ref[...] = jnp.exp(lw)
    g_ref[...] = g
    bonus_ref[...] = bonus


def _rwkv_step_prep(p_all, shift, mu, vec, lora, *, row0, col0):
    n, shift_w = shift.shape
    rw_w = RW_HEADS * RW_DIM
    nb = rw_w // LANES
    rb0 = row0 // n
    code_blk = (col0 + 3 * nb) // 2

    def pspec(seg):
        return pl.BlockSpec((n, LANES), lambda h: (rb0, col0 + seg * nb + h))

    def sspec(seg):
        return pl.BlockSpec((n, LANES), lambda h: (0, seg * nb + h))

    def mspec(seg):
        return pl.BlockSpec((1, LANES), lambda h: (0, seg * nb + h))

    ospec = pl.BlockSpec((n, LANES), lambda h: (0, h))
    mu2 = mu.reshape(1, shift_w)
    return pl.pallas_call(
        _rwkv_step_prep_kernel,
        out_shape=tuple(jax.ShapeDtypeStruct((n, rw_w), F32) for _ in range(8)),
        grid=(nb,),
        in_specs=[pspec(0), pspec(1), pspec(2), pl.BlockSpec((n, 2 * LANES), lambda h: (rb0, code_blk)),
                  sspec(0), sspec(1), sspec(2), pl.BlockSpec((n, 2 * LANES), lambda h: (0, 3 * nb // 2)),
                  mspec(0), mspec(1), mspec(2), pl.BlockSpec((1, 2 * LANES), lambda h: (0, 3 * nb // 2)),
                  pl.BlockSpec((8, LANES), lambda h: (0, h)), pl.BlockSpec((3, LANES, LANES), lambda h: (0, 0, h))],
        out_specs=tuple(ospec for _ in range(8)),
        compiler_params=_params("parallel"),
        name="rwkv_step_prep",
    )(p_all, p_all, p_all, p_all, shift, shift, shift, shift, mu2, mu2, mu2, mu2, vec, lora)


def _rwkv_step_kernel(s_ref, r_ref, k_ref, kk_ref, kka_ref, w_ref, vt_ref, so_ref, yt_ref):
    def head(h, carry):
        rows = pl.ds(pl.multiple_of(h * RW_DIM, RW_DIM), RW_DIM)
        for j in range(STEP_REQ):
            s = s_ref[j, h]
            sa = jnp.sum(s * kk_ref[j, pl.ds(h, 1), :], axis=1, keepdims=True)
            s_new = (s * w_ref[j, pl.ds(h, 1), :] - sa * kka_ref[j, pl.ds(h, 1), :]
                     + vt_ref[0, rows, j:j + 1] * k_ref[j, pl.ds(h, 1), :])
            so_ref[j, h] = s_new
            yt_ref[0, rows, j:j + 1] = jnp.sum(s_new * r_ref[j, pl.ds(h, 1), :], axis=1, keepdims=True)
        return carry

    lax.fori_loop(0, RW_HEADS, head, 0)


def _rwkv_step(state, r, k, kk, kka, w, v):
    n = state.shape[0]
    rw_w = RW_HEADS * RW_DIM
    nblk = n // STEP_REQ
    heads = lambda t: t.reshape(n, RW_HEADS, RW_DIM)
    v_t = v.reshape(nblk, STEP_REQ, rw_w).transpose(0, 2, 1)
    sspec = pl.BlockSpec((STEP_REQ, RW_HEADS, RW_DIM, RW_DIM), lambda i: (i, 0, 0, 0))
    rspec = pl.BlockSpec((STEP_REQ, RW_HEADS, RW_DIM), lambda i: (i, 0, 0))
    cspec = pl.BlockSpec((1, rw_w, STEP_REQ), lambda i: (i, 0, 0))
    s_new, y_t = pl.pallas_call(
        _rwkv_step_kernel,
        out_shape=(jax.ShapeDtypeStruct(state.shape, F32), jax.ShapeDtypeStruct((nblk, rw_w, STEP_REQ), F32)),
        grid=(nblk,),
        in_specs=[sspec, rspec, rspec, rspec, rspec, rspec, cspec],
        out_specs=(sspec, cspec),
        compiler_params=_params("parallel", vmem=VMEM_LIMIT_BYTES),
        name="rwkv_step",
    )(state, heads(r), heads(k), heads(kk), heads(kka), heads(w), v_t)
    return y_t.transpose(0, 2, 1).reshape(n, rw_w), s_new


def _rwkv_step_post_kernel(y_ref, bonus_ref, g_ref, vec_ref, o_ref):
    o_ref[...] = _rw_post(y_ref[...], bonus_ref[...], g_ref[...], vec_ref[...], _head_ones()).astype(o_ref.dtype)


def _rwkv_step_post(y, bonus, g, vec):
    n, rw_w = y.shape
    spec = pl.BlockSpec((n, LANES), lambda h: (0, h))
    return pl.pallas_call(
        _rwkv_step_post_kernel,
        out_shape=jax.ShapeDtypeStruct((n, rw_w), BF16),
        grid=(rw_w // LANES,),
        in_specs=[spec, spec, spec, pl.BlockSpec((8, LANES), lambda h: (0, h))],
        out_specs=spec,
        compiler_params=_params("parallel"),
        name="rwkv_step_post",
    )(y, bonus, g, vec)


def _merge_kernel(oh_ref, or_ref, wh_ref, wr_ref, gh_ref, gr_ref, o_ref, whb_ref, wrb_ref):
    @pl.when(pl.program_id(1) == 0)
    def _():
        whb_ref[...] = wh_ref[...].astype(BF16)
        wrb_ref[...] = wr_ref[...].astype(BF16)

    m = (jax.nn.sigmoid(gh_ref[...]) * _nn(oh_ref[...], whb_ref[...])
         + jax.nn.sigmoid(gr_ref[...]) * _nn(or_ref[...], wrb_ref[...]))
    o_ref[...] = m.astype(o_ref.dtype)


def _merge(o_hg, o_rw, w_hg, w_rw, p_all, *, gate_col0, tm, tn):
    m, kh = o_hg.shape
    kr = o_rw.shape[1]
    n = w_hg.shape[1]
    gb = gate_col0 // tn
    return pl.pallas_call(
        _merge_kernel,
        out_shape=jax.ShapeDtypeStruct((m, n), BF16),
        grid=(n // tn, m // tm),
        in_specs=[pl.BlockSpec((tm, kh), lambda j, i: (i, 0)), pl.BlockSpec((tm, kr), lambda j, i: (i, 0)),
                  pl.BlockSpec((kh, tn), lambda j, i: (0, j)), pl.BlockSpec((kr, tn), lambda j, i: (0, j)),
                  pl.BlockSpec((tm, tn), lambda j, i: (i, gb + j)),
                  pl.BlockSpec((tm, tn), lambda j, i: (i, gb + n // tn + j))],
        out_specs=pl.BlockSpec((tm, tn), lambda j, i: (i, j)),
        scratch_shapes=[pltpu.VMEM((kh, tn), BF16), pltpu.VMEM((kr, tn), BF16)],
        compiler_params=_params("parallel", "arbitrary"),
        name="merge",
    )(o_hg, o_rw, w_hg, w_rw, p_all, p_all)


def _softmax_rows(s):
    e = jnp.exp(s - jnp.max(s, axis=-1, keepdims=True))
    return e / jnp.sum(e, axis=-1, keepdims=True)


def _attn_heads(q, kb_ref, vb_ref, o_ref, store):
    dh = q.shape[-1] // CA_HEADS
    scale = dh ** -0.5
    for h in range(CA_HEADS):
        hs = slice(h * dh, (h + 1) * dh)
        p = _softmax_rows(_nt(q[:, hs], kb_ref[:, hs]) * scale)
        store(hs, _nn(p.astype(BF16), vb_ref[:, hs]))


def _attn_prompt_kernel(q_ref, k_ref, v_ref, o_ref, kb_ref, vb_ref):
    @pl.when(pl.program_id(1) == 0)
    def _():
        kb_ref[...] = k_ref[0].astype(BF16)
        vb_ref[...] = v_ref[0].astype(BF16)

    def store(hs, o):
        o_ref[:, hs] = o.astype(o_ref.dtype)

    _attn_heads(q_ref[...], kb_ref, vb_ref, o_ref, store)


def _attn_prompt(q_all, mem_k, mem_v, *, batch, seq, tq):
    _, mt, d = mem_k.shape
    nl = seq // tq
    kv = pl.BlockSpec((1, mt, d), lambda b, l: (b, 0, 0))
    qs = pl.BlockSpec((tq, d), lambda b, l: (b * nl + l, 0))
    return pl.pallas_call(
        _attn_prompt_kernel,
        out_shape=jax.ShapeDtypeStruct((batch * seq, d), BF16),
        grid=(batch, nl),
        in_specs=[qs, kv, kv],
        out_specs=qs,
        scratch_shapes=[pltpu.VMEM((mt, d), BF16), pltpu.VMEM((mt, d), BF16)],
        compiler_params=_params("parallel", "arbitrary"),
        name="attn_prompt",
    )(q_all, mem_k, mem_v)


def _attn_step_kernel(q_ref, k_ref, v_ref, o_ref, kb_ref, vb_ref):
    kb_ref[...] = k_ref[0].astype(BF16)
    vb_ref[...] = v_ref[0].astype(BF16)
    q = jnp.broadcast_to(q_ref[0], (8, q_ref.shape[-1]))

    def store(hs, o):
        o_ref[0, :, hs] = o[0:1].astype(o_ref.dtype)

    _attn_heads(q, kb_ref, vb_ref, o_ref, store)


def _attn_step(q, mem_k, mem_v):
    n, mt, d = mem_k.shape
    kv = pl.BlockSpec((1, mt, d), lambda b: (b, 0, 0))
    qs = pl.BlockSpec((1, 1, d), lambda b: (b, 0, 0))
    o = pl.pallas_call(
        _attn_step_kernel,
        out_shape=jax.ShapeDtypeStruct((n, 1, d), BF16),
        grid=(n,),
        in_specs=[qs, kv, kv],
        out_specs=qs,
        scratch_shapes=[pltpu.VMEM((mt, d), BF16), pltpu.VMEM((mt, d), BF16)],
        compiler_params=_params("parallel"),
        name="attn_step",
    )(q.reshape(n, 1, d), mem_k, mem_v)
    return o.reshape(n, d)


_RT_E1, _RT_E2, _RT_C1, _RT_C2 = range(4)


def _router_kernel(x_ref, g_ref, wr_ref, br_ref, h_ref, rt_ref):
    x = x_ref[...]
    h = x * lax.rsqrt(jnp.mean(x * x, axis=-1, keepdims=True) + RMS_EPS) * g_ref[...]
    h_ref[...] = h
    logits = _dot3(h, wr_ref[...]) + br_ref[...]
    lane = _iota2(logits.shape, 1)
    lane_f = lane.astype(F32)
    neg = -jnp.inf
    first = lambda hit: jnp.min(jnp.where(hit, lane_f, float(LANES)), axis=-1, keepdims=True)

    is_group = lane < MOE_GROUPS
    gl = jnp.where(is_group, logits, neg)
    g_max = jnp.max(gl, axis=-1, keepdims=True)
    g_top = first(gl == g_max)
    g_w = 1.0 / jnp.sum(jnp.where(is_group, jnp.exp(logits - g_max), 0.0), axis=-1, keepdims=True)

    e_idx = lane - MOE_GROUPS
    in_group = (e_idx >= 0) & (e_idx < MOE_EXPERTS) & (jnp.right_shift(e_idx, MOE_PER_GROUP.bit_length() - 1).astype(F32) == g_top)
    el = jnp.where(in_group, logits, neg)
    m1 = jnp.max(el, axis=-1, keepdims=True)
    i1 = first(el == m1)
    el2 = jnp.where(lane_f == i1, neg, el)
    m2 = jnp.max(el2, axis=-1, keepdims=True)
    i2 = first(el2 == m2)
    t = jnp.exp(m2 - m1)
    w1 = 1.0 / (1.0 + t)
    w2 = t / (1.0 + t)
    rec = jnp.where(lane == _RT_E1, i1 - MOE_GROUPS, 0.0)
    rec = jnp.where(lane == _RT_E2, i2 - MOE_GROUPS, rec)
    rec = jnp.where(lane == _RT_C1, g_w * w1, rec)
    rec = jnp.where(lane == _RT_C2, g_w * w2, rec)
    rt_ref[...] = rec


def _router(x, g, w_rg, b_rg, w_re, b_re, *, tm):
    t, d = x.shape
    pad = LANES - MOE_GROUPS - MOE_EXPERTS
    wr = jnp.concatenate([w_rg, w_re, jnp.zeros((d, pad), F32)], axis=1)
    br = jnp.concatenate([b_rg, b_re, jnp.zeros((pad,), F32)]).reshape(1, LANES)
    return pl.pallas_call(
        _router_kernel,
        out_shape=(jax.ShapeDtypeStruct((t, d), F32), jax.ShapeDtypeStruct((t, LANES), F32)),
        grid=(t // tm,),
        in_specs=[pl.BlockSpec((tm, d), lambda i: (i, 0)), pl.BlockSpec((1, d), lambda i: (0, 0)),
                  pl.BlockSpec((d, LANES), lambda i: (0, 0)), pl.BlockSpec((1, LANES), lambda i: (0, 0))],
        out_specs=(pl.BlockSpec((tm, d), lambda i: (i, 0)), pl.BlockSpec((tm, LANES), lambda i: (i, 0))),
        compiler_params=_params("parallel", vmem=VMEM_LIMIT_BYTES),
        name="moe_router",
    )(x, g.reshape(1, d), wr, br)


def _dispatch_plan(e1, e2, n_tiles):
    t = e1.shape[0]
    keys = jnp.concatenate([e1, e2])
    onehot = (keys[:, None] == jnp.arange(MOE_EXPERTS, dtype=jnp.int32)[None, :]).astype(jnp.int32)
    rank = jnp.take_along_axis(jnp.cumsum(onehot, axis=0), keys[:, None], axis=1)[:, 0] - 1
    counts = jnp.sum(onehot, axis=0)
    tiles = (counts + MOE_TILE - 1) // MOE_TILE
    tile_end = jnp.cumsum(tiles)
    tile_start = tile_end - tiles
    dest = tile_start[keys] * MOE_TILE + rank
    token = jnp.concatenate([jnp.arange(t, dtype=jnp.int32)] * 2)
    row_src = jnp.zeros((n_tiles * MOE_TILE,), jnp.int32).at[dest].set(token)
    n_used = tile_end[-1].astype(jnp.int32)
    tile_id = jnp.arange(n_tiles, dtype=jnp.int32)
    tile_expert = jnp.sum(tile_id[:, None] >= tile_end[None, :], axis=1).astype(jnp.int32)
    tile_expert = jnp.minimum(tile_expert, tile_expert[jnp.maximum(n_used - 1, 0)])
    return row_src, tile_expert, n_used.reshape(1), dest.reshape(2, t).astype(jnp.int32)


def _expert_kernel(te_ref, nu_ref, src_ref, h_hbm, w1_ref, w3_ref, w2_ref, o_ref,
                   xbuf, sem, w1b, w3b, w2b):
    i = pl.program_id(0)
    n_used = nu_ref[0]

    def gather(tile, slot):
        def body(r, carry):
            row = src_ref[tile * MOE_TILE + r]
            pltpu.make_async_copy(h_hbm.at[pl.ds(row, 1)], xbuf.at[slot, pl.ds(r, 1)], sem.at[slot]).start()
            return carry
        lax.fori_loop(0, MOE_TILE, body, 0)

    @pl.when(i == 0)
    def _():
        gather(0, 0)

    @pl.when(i + 1 < n_used)
    def _():
        gather(i + 1, (i + 1) % 2)

    @pl.when(i < n_used)
    def _():
        slot = i % 2
        pltpu.make_async_copy(h_hbm.at[pl.ds(0, MOE_TILE)], xbuf.at[slot], sem.at[slot]).wait()

        @pl.when((i == 0) | (te_ref[i] != te_ref[jnp.maximum(i - 1, 0)]))
        def _():
            w1b[...] = w1_ref[0].astype(BF16)
            w3b[...] = w3_ref[0].astype(BF16)
            w2b[...] = w2_ref[0].astype(BF16)

        x = xbuf[slot].astype(BF16)
        hid = _silu(_nn(x, w1b[...])) * _nn(x, w3b[...])
        o_ref[...] = _nn(hid.astype(BF16), w2b[...])

    @pl.when(i >= n_used)
    def _():
        o_ref[...] = jnp.zeros_like(o_ref)


def _experts(h, w1, w3, w2, row_src, tile_expert, n_used):
    t, d = h.shape
    f = w1.shape[-1]
    n_tiles = tile_expert.shape[0]
    grid_spec = pltpu.PrefetchScalarGridSpec(
        num_scalar_prefetch=3,
        grid=(n_tiles,),
        in_specs=[pl.BlockSpec(memory_space=pl.ANY),
                  pl.BlockSpec((1, d, f), lambda i, te, nu, src: (te[i], 0, 0)),
                  pl.BlockSpec((1, d, f), lambda i, te, nu, src: (te[i], 0, 0)),
                  pl.BlockSpec((1, f, d), lambda i, te, nu, src: (te[i], 0, 0))],
        out_specs=pl.BlockSpec((MOE_TILE, d), lambda i, te, nu, src: (i, 0)),
        scratch_shapes=[pltpu.VMEM((2, MOE_TILE, d), F32), pltpu.SemaphoreType.DMA((2,)),
                        pltpu.VMEM((d, f), BF16), pltpu.VMEM((d, f), BF16), pltpu.VMEM((f, d), BF16)],
    )
    return pl.pallas_call(
        _expert_kernel,
        out_shape=jax.ShapeDtypeStruct((n_tiles * MOE_TILE, d), F32),
        grid_spec=grid_spec,
        compiler_params=_params("arbitrary", vmem=VMEM_LIMIT_BYTES),
        name="moe_experts",
    )(tile_expert, n_used, row_src, h, w1, w3, w2)


def _combine_kernel(pos_ref, x_ref, rt_ref, g_ref, ys_hbm, o_ref, ybuf, sem, *, tok0, n_tok):
    i = pl.program_id(0)
    tm = x_ref.shape[0]

    def gather(tile, slot):
        def body(r, carry):
            for s in range(2):
                row = pos_ref[s * n_tok + tok0 + tile * tm + r]
                pltpu.make_async_copy(ys_hbm.at[pl.ds(row, 1)], ybuf.at[slot, s, pl.ds(r, 1)], sem.at[slot]).start()
            return carry
        lax.fori_loop(0, tm, body, 0)

    @pl.when(i == 0)
    def _():
        gather(0, 0)

    @pl.when(i + 1 < pl.num_programs(0))
    def _():
        gather(i + 1, (i + 1) % 2)

    slot = i % 2
    for s in range(2):
        pltpu.make_async_copy(ys_hbm.at[pl.ds(0, tm)], ybuf.at[slot, s], sem.at[slot]).wait()
    rt = rt_ref[...]
    y = x_ref[...] + rt[:, _RT_C1:_RT_C1 + 1] * ybuf[slot, 0] + rt[:, _RT_C2:_RT_C2 + 1] * ybuf[slot, 1]
    o_ref[...] = y * lax.rsqrt(jnp.mean(y * y, axis=-1, keepdims=True) + RMS_EPS) * g_ref[...]


def _combine(x_all, rt, g, ys, pos, *, tok0, count, tm):
    n_tok, d = x_all.shape
    b0 = tok0 // tm
    grid_spec = pltpu.PrefetchScalarGridSpec(
        num_scalar_prefetch=1,
        grid=(count // tm,),
        in_specs=[pl.BlockSpec((tm, d), lambda i, pos: (b0 + i, 0)),
                  pl.BlockSpec((tm, LANES), lambda i, pos: (b0 + i, 0)),
                  pl.BlockSpec((1, d), lambda i, pos: (0, 0)),
                  pl.BlockSpec(memory_space=pl.ANY)],
        out_specs=pl.BlockSpec((tm, d), lambda i, pos: (i, 0)),
        scratch_shapes=[pltpu.VMEM((2, 2, tm, d), F32), pltpu.SemaphoreType.DMA((2,))],
    )
    return pl.pallas_call(
        functools.partial(_combine_kernel, tok0=tok0, n_tok=n_tok),
        out_shape=jax.ShapeDtypeStruct((count, d), F32),
        grid_spec=grid_spec,
        compiler_params=_params("arbitrary"),
        name="moe_combine",
    )(pos.reshape(-1), x_all, rt, g.reshape(1, d), ys)


def kernel(x_prompt, x_sample, mem_prompt, cache_mem_k, cache_mem_v, state_hgrn, state_rwkv, state_rwkv_shift, g_mix, w_in, hg_lb_logits, hg_norm_g, rw_mu, rw_w0, rw_w2, rw_a0, rw_a2, rw_g2, rw_kk, rw_ka, rw_rk, rw_ln_g, rw_ln_b, w_br_hg, w_br_rw, w_out, g_ca, g_mem, w_ca_q, w_ca_k, w_ca_v, w_ca_o, g_moe, w_rg, b_rg, w_re, b_re, w_e1, w_e3, w_e2, g_final):
    bp, seq, d = x_prompt.shape
    ns = x_sample.shape[0]
    tp = bp * seq
    t_all = tp + ns
    assert w_in.shape[0] == 1, "the final RMSNorm is fused into the last layer's MoE combine"
    layer = 0
    hg_w = HG_HEADS * HG_DIM
    rw_w = RW_HEADS * RW_DIM
    shift_w = state_rwkv_shift.shape[-1]
    rw_col = 4 * hg_w
    gate_col = rw_col + shift_w
    mem_t = mem_prompt.shape[1]
    tm = 640
    assert t_all % tm == 0

    lb = jnp.cumsum(jax.nn.softmax(hg_lb_logits.astype(F32), axis=0), axis=0)[layer]
    x_all = jnp.concatenate([x_prompt.reshape(tp, d), x_sample.reshape(ns, d)])

    h = _rmsnorm(x_all, g_mix[layer], BF16, tm)
    p_all = _matmul(h, w_in[layer], tm=tm, tn=1280, out_dtype=F32)
    gn = hg_norm_g[layer]
    o_hg_p, hg_p = _hgrn_prompt(p_all, lb, gn, batch=bp, seq=seq, col0=0, lblk=512)
    o_hg_s, hg_s = _hgrn_step(p_all, state_hgrn[layer], lb, gn, row0=tp, col0=0)
    P = {'rw_w0': rw_w0[layer], 'rw_a0': rw_a0[layer], 'rw_kk': rw_kk[layer], 'rw_ka': rw_ka[layer],
         'rw_rk': rw_rk[layer], 'rw_ln_g': rw_ln_g[layer], 'rw_ln_b': rw_ln_b[layer],
         'rw_w2': rw_w2[layer], 'rw_a2': rw_a2[layer], 'rw_g2': rw_g2[layer]}
    vec, lora = _rw_param_blocks(P)
    mu = rw_mu[layer]
    o_rw_p, rw_p = _rwkv_prompt(p_all, jnp.zeros((bp, shift_w), F32), mu, vec, lora,
                                batch=bp, seq=seq, col0=rw_col // LANES, lblk=256)
    r_s, k_s, v_s, kk_s, kka_s, w_s, g_s, bonus_s = _rwkv_step_prep(
        p_all, state_rwkv_shift[layer], mu, vec, lora, row0=tp, col0=rw_col // LANES)
    y_s, rw_s = _rwkv_step(state_rwkv[layer], r_s, k_s, kk_s, kka_s, w_s, v_s)
    o_rw_s = _rwkv_step_post(y_s, bonus_s, g_s, vec)
    shift_p = p_all[:tp].reshape(bp, seq, -1)[:, seq - 1, rw_col:gate_col]
    shift_s = p_all[tp:, rw_col:gate_col]
    merged = _merge(jnp.concatenate([o_hg_p, o_hg_s]), jnp.concatenate([o_rw_p, o_rw_s]),
                    w_br_hg[layer], w_br_rw[layer], p_all, gate_col0=gate_col, tm=tm, tn=256)
    x1 = _matmul(merged, w_out[layer], tm=tm, tn=512, out_dtype=F32, res=x_all)

    q = _matmul(_rmsnorm(x1, g_ca[layer], BF16, tm), w_ca_q[layer], tm=tm, tn=512, out_dtype=BF16)
    m = _rmsnorm(mem_prompt.reshape(bp * mem_t, d), g_mem[layer], BF16, mem_t)
    mem_k = _matmul(m, w_ca_k[layer], tm=bp * mem_t, tn=512, out_dtype=F32)
    mem_v = _matmul(m, w_ca_v[layer], tm=bp * mem_t, tn=512, out_dtype=F32)
    o_p = _attn_prompt(q, mem_k.reshape(bp, mem_t, d), mem_v.reshape(bp, mem_t, d), batch=bp, seq=seq, tq=512)
    o_s = _attn_step(q[tp:], cache_mem_k[layer].reshape(ns, mem_t, d), cache_mem_v[layer].reshape(ns, mem_t, d))
    x2 = _matmul(jnp.concatenate([o_p, o_s]), w_ca_o[layer], tm=tm, tn=512, out_dtype=F32, res=x1)

    h3, rt = _router(x2, g_moe[layer], w_rg[layer], b_rg[layer], w_re[layer], b_re[layer], tm=tm)
    n_tiles = (2 * t_all + MOE_EXPERTS * (MOE_TILE - 1)) // MOE_TILE
    row_src, tile_expert, n_used, pos = _dispatch_plan(
        rt[:, _RT_E1].astype(jnp.int32), rt[:, _RT_E2].astype(jnp.int32), n_tiles)
    ys = _experts(h3, w_e1[layer], w_e3[layer], w_e2[layer], row_src, tile_expert, n_used)
    y_p = _combine(x2, rt, g_final, ys, pos, tok0=0, count=tp, tm=LANES)
    y_s = _combine(x2, rt, g_final, ys, pos, tok0=tp, count=ns, tm=LANES)

    ca_heads_shape = (1, bp, mem_t, CA_HEADS, d // CA_HEADS)
    return (y_p.reshape(bp, seq, d), y_s.reshape(ns, 1, d), hg_p[None], rw_p[None], shift_p[None],
            mem_k.reshape(ca_heads_shape), mem_v.reshape(ca_heads_shape), hg_s[None], rw_s[None], shift_s[None])
```

```python
import functools

import jax
import jax.numpy as jnp
from jax import lax
from jax.experimental import pallas as pl
from jax.experimental.pallas import tpu as pltpu

F32 = jnp.float32
BF16 = jnp.bfloat16

RMS_EPS = 1e-6
GN_EPS = 64e-5

LANES = 128
VMEM_LIMIT_BYTES = 56 * 1024 * 1024

HG_HEADS = 8
HG_DIM = 128
HG_CHUNK = 64
RW_HEADS = 16
RW_DIM = 64
RW_CHUNK = 64
RW_SUB = 16
CA_HEADS = 4
MOE_GROUPS = 4
MOE_PER_GROUP = 8
MOE_EXPERTS = MOE_GROUPS * MOE_PER_GROUP
MOE_TILE = 256


def _params(*sem, vmem=None):
    return pltpu.CompilerParams(dimension_semantics=sem, vmem_limit_bytes=vmem)


def _nt(a, b):
    return lax.dot_general(a, b, (((1,), (1,)), ((), ())), preferred_element_type=F32)


def _tn(a, b):
    return lax.dot_general(a, b, (((0,), (0,)), ((), ())), preferred_element_type=F32)


def _nn(a, b):
    return jnp.dot(a, b, preferred_element_type=F32)


def _split2(x):
    hi = x.astype(BF16)
    lo = (x - hi.astype(F32)).astype(BF16)
    return hi, lo


def _split3(x):
    hi = x.astype(BF16)
    r = x - hi.astype(F32)
    mid = r.astype(BF16)
    lo = (r - mid.astype(F32)).astype(BF16)
    return hi, mid, lo


def _dot3(a, b, dot=_nn):
    ah, al = _split2(a)
    bh, bl = _split2(b)
    return dot(ah, bh) + (dot(ah, bl) + dot(al, bh))


def _dot_exact_rhs(a, b_bf16, dot=_nn):
    h, m, l = _split3(a)
    return dot(h, b_bf16) + (dot(m, b_bf16) + dot(l, b_bf16))


def _iota2(shape, dim):
    return lax.broadcasted_iota(jnp.int32, shape, dim)


def _silu(x):
    return x * jax.nn.sigmoid(x)


def _rms_kernel(x_ref, g_ref, o_ref):
    x = x_ref[...]
    ms = jnp.mean(x * x, axis=-1, keepdims=True)
    o_ref[...] = (x * lax.rsqrt(ms + RMS_EPS) * g_ref[...]).astype(o_ref.dtype)


def _rmsnorm(x, g, out_dtype, tm):
    t, d = x.shape
    return pl.pallas_call(
        _rms_kernel,
        out_shape=jax.ShapeDtypeStruct((t, d), out_dtype),
        grid=(t // tm,),
        in_specs=[pl.BlockSpec((tm, d), lambda i: (i, 0)), pl.BlockSpec((1, d), lambda i: (0, 0))],
        out_specs=pl.BlockSpec((tm, d), lambda i: (i, 0)),
        compiler_params=_params("parallel"),
        name="rmsnorm",
    )(x, g.reshape(1, d))


def _mm_kernel(x_ref, w_ref, *rest, has_res):
    if has_res:
        r_ref, o_ref, wb_ref = rest
    else:
        o_ref, wb_ref = rest

    @pl.when(pl.program_id(1) == 0)
    def _():
        wb_ref[...] = w_ref[...].astype(BF16)

    acc = _nn(x_ref[...], wb_ref[...])
    if has_res:
        acc = r_ref[...] + acc
    o_ref[...] = acc.astype(o_ref.dtype)


def _matmul(x, w, *, tm, tn, out_dtype, res=None):
    m, k = x.shape
    n = w.shape[1]
    in_specs = [pl.BlockSpec((tm, k), lambda j, i: (i, 0)), pl.BlockSpec((k, tn), lambda j, i: (0, j))]
    args = [x, w]
    if res is not None:
        in_specs.append(pl.BlockSpec((tm, tn), lambda j, i: (i, j)))
        args.append(res)
    return pl.pallas_call(
        functools.partial(_mm_kernel, has_res=res is not None),
        out_shape=jax.ShapeDtypeStruct((m, n), out_dtype),
        grid=(n // tn, m // tm),
        in_specs=in_specs,
        out_specs=pl.BlockSpec((tm, tn), lambda j, i: (i, j)),
        scratch_shapes=[pltpu.VMEM((k, tn), BF16)],
        compiler_params=_params("parallel", "arbitrary", vmem=VMEM_LIMIT_BYTES),
        name="matmul",
    )(*args)


def _hgrn_gates(hq, hf, lb):
    q = _silu(hq)
    f = lb + (1.0 - lb) * jax.nn.sigmoid(hf)
    return q, f


def _hgrn_out(o, gn, hog):
    ms = jnp.mean(o * o, axis=-1, keepdims=True)
    return o * lax.rsqrt(ms + RMS_EPS) * gn * _silu(hog)


def _hgrn_prompt_kernel(q_ref, f_ref, i_ref, og_ref, lb_ref, gn_ref, o_ref, s_ref, st_ref, *, n_chunks):
    l = pl.program_id(2)
    c = HG_CHUNK

    @pl.when(l == 0)
    def _():
        st_ref[...] = jnp.zeros_like(st_ref)

    lb = lb_ref[...]
    gn = gn_ref[...]
    incl = _iota2((c, c), 0) >= _iota2((c, c), 1)
    tri = incl.astype(BF16)
    for ci in range(n_chunks):
        sl = slice(ci * c, (ci + 1) * c)
        q, f = _hgrn_gates(q_ref[sl, :], f_ref[sl, :], lb)
        k = 1.0 - f
        vb = i_ref[sl, :].astype(BF16)
        b = _cumsum_rows(jnp.log(f), tri)
        b_last = b[c - 1:c, :]
        q_in = (q * jnp.exp(b)).astype(BF16)
        k_in = (k * jnp.exp(-b)).astype(BF16)
        k_out = (k * jnp.exp(b_last - b)).astype(BF16)
        st = st_ref[...]
        o_inter = _nt(q_in, st.astype(BF16))
        att = jnp.where(incl, _nt(q_in, k_in), 0.0)
        o_intra = _nn(att.astype(BF16), vb)
        st_ref[...] = st * jnp.exp(b_last) + _tn(vb, k_out)
        o_ref[sl, :] = _hgrn_out(o_inter + o_intra, gn, og_ref[sl, :]).astype(o_ref.dtype)

    @pl.when(l == pl.num_programs(2) - 1)
    def _():
        s_ref[0, 0] = st_ref[...].T


def _cumsum_rows(x, tri_bf16):
    h, m, l = _split3(x)
    return _nn(tri_bf16, h) + (_nn(tri_bf16, m) + _nn(tri_bf16, l))


def _hgrn_prompt(p_all, lb, gn, *, batch, seq, col0, lblk):
    nl = seq // lblk
    hw = HG_HEADS * HG_DIM
    nb = hw // LANES

    def pspec(seg):
        return pl.BlockSpec((lblk, LANES), lambda b, h, l: (b * nl + l, col0 + seg * nb + h))

    vec = pl.BlockSpec((1, LANES), lambda b, h, l: (0, h))
    return pl.pallas_call(
        functools.partial(_hgrn_prompt_kernel, n_chunks=lblk // HG_CHUNK),
        out_shape=(jax.ShapeDtypeStruct((batch * seq, hw), BF16),
                   jax.ShapeDtypeStruct((batch, HG_HEADS, HG_DIM, HG_DIM), F32)),
        grid=(batch, HG_HEADS, nl),
        in_specs=[pspec(0), pspec(1), pspec(2), pspec(3), vec, vec],
        out_specs=(pl.BlockSpec((lblk, LANES), lambda b, h, l: (b * nl + l, h)),
                   pl.BlockSpec((1, 1, HG_DIM, HG_DIM), lambda b, h, l: (b, h, 0, 0))),
        scratch_shapes=[pltpu.VMEM((HG_DIM, HG_DIM), F32)],
        compiler_params=_params("parallel", "parallel", "arbitrary"),
        name="hgrn_prompt",
    )(p_all, p_all, p_all, p_all, lb.reshape(1, hw), gn.reshape(1, hw))


_RW_W0, _RW_A0, _RW_KK, _RW_KA, _RW_RK, _RW_LNG, _RW_LNB = range(7)


def _head_ones():
    same = (_iota2((LANES, LANES), 0) // RW_DIM) == (_iota2((LANES, LANES), 1) // RW_DIM)
    return same.astype(BF16)


def _head_sum(x, ones_bd):
    return _dot_exact_rhs(x, ones_bd)


def _rw_prep(xr, xk, xv, xc, vec, lora, ones_bd):
    c1 = xc[:, :LANES]
    w_lora = _nn(jnp.tanh(c1).astype(BF16), lora[0].astype(BF16))
    a_lora = _nn(c1.astype(BF16), lora[1].astype(BF16))
    g = _nn(jax.nn.sigmoid(xc[:, LANES:]).astype(BF16), lora[2].astype(BF16))
    z = -(vec[_RW_W0:_RW_W0 + 1] + w_lora)
    softplus = jnp.maximum(z, 0.0) + jnp.log1p(jnp.exp(-jnp.abs(z)))
    lw = -jnp.exp(-softplus - 0.5)
    a = jax.nn.sigmoid(vec[_RW_A0:_RW_A0 + 1] + a_lora)
    kk = xk * vec[_RW_KK:_RW_KK + 1]
    kk = kk / jnp.maximum(jnp.sqrt(_head_sum(kk * kk, ones_bd)), 1e-12)
    k = xk * (1.0 + (a - 1.0) * vec[_RW_KA:_RW_KA + 1])
    bonus = _head_sum(xr * k * vec[_RW_RK:_RW_RK + 1], ones_bd) * xv
    return xr, k, xv, kk, kk * a, lw, g, bonus


def _rw_post(y, bonus, g, vec, ones_bd):
    inv = 1.0 / RW_DIM
    mu = _head_sum(y, ones_bd) * inv
    d = y - mu
    var = _head_sum(d * d, ones_bd) * inv
    yn = d * lax.rsqrt(var + GN_EPS) * vec[_RW_LNG:_RW_LNG + 1] + vec[_RW_LNB:_RW_LNB + 1]
    return (yn + bonus) * g


def _rw_chunk_operands(r, k, v, kk, kka, lw, tri):
    c = r.shape[0]
    ginc = _cumsum_rows(lw, tri)
    glast = ginc[c - 1:c, :]
    e_neg = jnp.exp(-ginc)
    e_rem = jnp.exp(glast - ginc)
    a_hat = kk * jnp.exp(ginc - lw)
    r_hat = r * jnp.exp(ginc)
    e_last = jnp.exp(glast)
    b_hat, k_hat, b_til, k_til = kka * e_neg, k * e_neg, kka * e_rem, k * e_rem
    units = []
    for h in range(LANES // RW_DIM):
        s = slice(h * RW_DIM, (h + 1) * RW_DIM)
        units.append(dict(
            ah=a_hat[:, s], rh=r_hat[:, s], ab=a_hat[:, s].astype(BF16), rb=r_hat[:, s].astype(BF16),
            bb=b_hat[:, s].astype(BF16), kb=k_hat[:, s].astype(BF16), bt=b_til[:, s].astype(BF16),
            kt=k_til[:, s].astype(BF16), vb=v[:, s].astype(BF16), e_last=e_last[:, s]))
    return units


def _rw_local_levels(us, out):
    c = us[0]['ab'].shape[0]
    row, col = _iota2((c, c), 0), _iota2((c, c), 1)
    incl, strict = row >= col, row > col
    diag_blk = (row // RW_SUB) == (col // RW_SUB)
    eye_c = (row == col).astype(F32)
    eye_d = _iota2((RW_DIM, RW_DIM), 0) == _iota2((RW_DIM, RW_DIM), 1)

    lm = [jnp.where(strict, _nt(u['ab'], u['bb']), 0.0) for u in us]
    lk = [jnp.where(strict, _nt(u['ab'], u['kb']), 0.0).astype(BF16) for u in us]
    pb = [jnp.where(incl, _nt(u['rb'], u['bb']), 0.0).astype(BF16) for u in us]
    pk = [jnp.where(incl, _nt(u['rb'], u['kb']), 0.0).astype(BF16) for u in us]
    yield
    pw = [jnp.where(diag_blk, -x, 0.0) for x in lm]
    e = [jnp.where(diag_blk, 0.0, x) for x in lm]
    t_d = [eye_c + x for x in pw]
    x = [jnp.concatenate([u['ah'], _nn(lk_, u['vb'])], axis=1) for u, lk_ in zip(us, lk)]
    for _ in range(RW_SUB.bit_length() - 2):
        pw = [_dot3(p, p) for p in pw]
        yield
        t_d = [t + _dot3(t, p) for t, p in zip(t_d, pw)]
    yield
    m = [_dot3(t, e_) for t, e_ in zip(t_d, e)]
    tdx = [_dot3(t, x_) for t, x_ in zip(t_d, x)]
    yield
    acc = [eye_c - m_ for m_ in m]
    pw = m
    for _ in range((c // RW_SUB).bit_length() - 2):
        pw = [_dot3(p, p) for p in pw]
        yield
        acc = [a + _dot3(a, p) for a, p in zip(acc, pw)]
        yield
    txb = [_dot3(a, t).astype(BF16) for a, t in zip(acc, tdx)]
    yield
    p_au = [_nn(p, t) for p, t in zip(pb, txb)]
    b_au = [_tn(u['bt'], t) for u, t in zip(us, txb)]
    pkv = [_nn(p, u['vb']) for u, p in zip(us, pk)]
    ktv = [_tn(u['kt'], u['vb']) for u in us]
    yield
    for i, u in enumerate(us):
        out.append((u['rh'] - p_au[i][:, :RW_DIM], pkv[i] - p_au[i][:, RW_DIM:],
                    jnp.where(eye_d, u['e_last'], 0.0) - b_au[i][:, :RW_DIM], ktv[i] - b_au[i][:, RW_DIM:]))


RW_GROUP_CHUNKS = 4


def _rwkv_prompt_kernel(pr_ref, pk_ref, pv_ref, pc_ref, sr_ref, sk_ref, sv_ref, sc_ref,
                        mr_ref, mk_ref, mv_ref, mc_ref, vec_ref, lora_ref,
                        o_ref, s_ref, a_ref, y_ref, cr_ref, ck_ref, cv_ref, cc_ref, *, n_chunks):
    l = pl.program_id(2)
    lblk = pr_ref.shape[0]
    c = RW_CHUNK

    @pl.when(l == 0)
    def _():
        a_ref[...] = jnp.zeros_like(a_ref)
        cr_ref[...] = sr_ref[0]
        ck_ref[...] = sk_ref[0]
        cv_ref[...] = sv_ref[0]
        cc_ref[...] = sc_ref[0]

    def shift_mix(x_ref, carry_ref, mu_ref):
        x = x_ref[...]
        first = _iota2(x.shape, 0) == 0
        prev = jnp.where(first, carry_ref[...], pltpu.roll(x, 1, 0))
        carry_ref[...] = x[lblk - 1:lblk, :]
        return x + mu_ref[...] * (prev - x)

    ones_bd = _head_ones()
    vec = vec_ref[...]
    r, k, v, kk, kka, lw, g, bonus = _rw_prep(
        shift_mix(pr_ref, cr_ref, mr_ref), shift_mix(pk_ref, ck_ref, mk_ref), shift_mix(pv_ref, cv_ref, mv_ref),
        shift_mix(pc_ref, cc_ref, mc_ref), vec, lora_ref[...], ones_bd)
    tri = (_iota2((c, c), 0) >= _iota2((c, c), 1)).astype(BF16)
    n_heads = LANES // RW_DIM
    state = [a_ref[h] for h in range(n_heads)]

    def state_step(ci, local):
        ys = []
        for h in range(n_heads):
            r_til, y_loc, g_mat, a_loc = local[h]
            ys.append(_dot3(r_til, state[h]) + y_loc)
            state[h] = _dot3(g_mat, state[h]) + a_loc
        y_ref[ci * c:(ci + 1) * c, :] = jnp.concatenate(ys, axis=1)

    pending = []
    for g0 in range(0, n_chunks, RW_GROUP_CHUNKS):
        chunks = range(g0, min(g0 + RW_GROUP_CHUNKS, n_chunks))
        units = []
        for ci in chunks:
            sl = slice(ci * c, (ci + 1) * c)
            units += _rw_chunk_operands(r[sl], k[sl], v[sl], kk[sl], kka[sl], lw[sl], tri)
        local = []
        for _ in _rw_local_levels(units, local):
            if pending:
                pending.pop(0)()
        pending += [functools.partial(state_step, ci, local[i * n_heads:(i + 1) * n_heads])
                    for i, ci in enumerate(chunks)]
    for step in pending:
        step()
    for h in range(n_heads):
        a_ref[h] = state[h]
    o_ref[...] = _rw_post(y_ref[...], bonus, g, vec, ones_bd).astype(o_ref.dtype)

    @pl.when(l == pl.num_programs(2) - 1)
    def _():
        for h in range(n_heads):
            s_ref[0, h] = state[h].T


def _rw_param_blocks(P):
    rw_w = RW_HEADS * RW_DIM
    vec = jnp.stack([P['rw_w0'], P['rw_a0'], P['rw_kk'], P['rw_ka'], P['rw_rk'], P['rw_ln_g'], P['rw_ln_b'],
                     jnp.zeros((rw_w,), F32)])
    z = jnp.zeros_like(P['rw_w2'])
    lora = jnp.stack([jnp.concatenate([P['rw_w2'], z]), jnp.concatenate([z, P['rw_a2']]), P['rw_g2']])
    return vec, lora


def _rwkv_prompt(p_all, shift0, mu, vec, lora, *, batch, seq, col0, lblk):
    nl = seq // lblk
    rw_w = RW_HEADS * RW_DIM
    nb = rw_w // LANES
    code_blk = (col0 + 3 * nb) // 2
    shift_w = shift0.shape[-1]

    def pspec(seg):
        return pl.BlockSpec((lblk, LANES), lambda b, h, l: (b * nl + l, col0 + seg * nb + h))

    def sspec(seg):
        return pl.BlockSpec((1, 1, LANES), lambda b, h, l: (b, 0, seg * nb + h))

    def mspec(seg):
        return pl.BlockSpec((1, LANES), lambda b, h, l: (0, seg * nb + h))

    in_specs = [pspec(0), pspec(1), pspec(2),
                pl.BlockSpec((lblk, 2 * LANES), lambda b, h, l: (b * nl + l, code_blk)),
                sspec(0), sspec(1), sspec(2),
                pl.BlockSpec((1, 1, 2 * LANES), lambda b, h, l: (b, 0, 3 * nb // 2)),
                mspec(0), mspec(1), mspec(2),
                pl.BlockSpec((1, 2 * LANES), lambda b, h, l: (0, 3 * nb // 2)),
                pl.BlockSpec((8, LANES), lambda b, h, l: (0, h)),
                pl.BlockSpec((3, LANES, LANES), lambda b, h, l: (0, 0, h))]
    sh3 = shift0.reshape(batch, 1, shift_w)
    mu2 = mu.reshape(1, shift_w)
    return pl.pallas_call(
        functools.partial(_rwkv_prompt_kernel, n_chunks=lblk // RW_CHUNK),
        out_shape=(jax.ShapeDtypeStruct((batch * seq, rw_w), BF16),
                   jax.ShapeDtypeStruct((batch, RW_HEADS, RW_DIM, RW_DIM), F32)),
        grid=(batch, nb, nl),
        in_specs=in_specs,
        out_specs=(pl.BlockSpec((lblk, LANES), lambda b, h, l: (b * nl + l, h)),
                   pl.BlockSpec((1, 2, RW_DIM, RW_DIM), lambda b, h, l: (b, h, 0, 0))),
        scratch_shapes=[pltpu.VMEM((2, RW_DIM, RW_DIM), F32), pltpu.VMEM((lblk, LANES), F32),
                        pltpu.VMEM((1, LANES), F32), pltpu.VMEM((1, LANES), F32), pltpu.VMEM((1, LANES), F32),
                        pltpu.VMEM((1, 2 * LANES), F32)],
        compiler_params=_params("parallel", "parallel", "arbitrary"),
        name="rwkv_prompt",
    )(p_all, p_all, p_all, p_all, sh3, sh3, sh3, sh3, mu2, mu2, mu2, mu2, vec, lora)


STEP_REQ = 16


def _transpose_rows(x):
    r = x.shape[0]
    eye = (_iota2((r, r), 0) == _iota2((r, r), 1)).astype(BF16)
    return _dot_exact_rhs(x, eye, dot=_tn)


def _hgrn_step_kernel(q_ref, f_ref, i_ref, og_ref, lb_ref, gn_ref, s_ref, o_ref, so_ref):
    q, f = _hgrn_gates(q_ref[...], f_ref[...], lb_ref[...])
    f_t = _transpose_rows(f)
    q_t = _transpose_rows(q)
    v = i_ref[...]
    rows = []
    for j in range(STEP_REQ):
        fc = f_t[:, j:j + 1]
        s_new = s_ref[j, 0] * fc + (1.0 - fc) * v[j:j + 1, :]
        so_ref[j, 0] = s_new
        rows.append(jnp.sum(s_new * q_t[:, j:j + 1], axis=0, keepdims=True))
    o = jnp.concatenate(rows, axis=0)
    o_ref[...] = _hgrn_out(o, gn_ref[...], og_ref[...]).astype(o_ref.dtype)


def _hgrn_step(p_all, state, lb, gn, *, row0, col0):
    n = state.shape[0]
    hw = HG_HEADS * HG_DIM
    nb = hw // LANES
    rb0 = row0 // STEP_REQ

    def pspec(seg):
        return pl.BlockSpec((STEP_REQ, LANES), lambda i, h: (rb0 + i, col0 + seg * nb + h))

    vec = pl.BlockSpec((1, LANES), lambda i, h: (0, h))
    sspec = pl.BlockSpec((STEP_REQ, 1, HG_DIM, HG_DIM), lambda i, h: (i, h, 0, 0))
    return pl.pallas_call(
        _hgrn_step_kernel,
        out_shape=(jax.ShapeDtypeStruct((n, hw), BF16), jax.ShapeDtypeStruct(state.shape, F32)),
        grid=(n // STEP_REQ, HG_HEADS),
        in_specs=[pspec(0), pspec(1), pspec(2), pspec(3), vec, vec, sspec],
        out_specs=(pl.BlockSpec((STEP_REQ, LANES), lambda i, h: (i, h)), sspec),
        compiler_params=_params("parallel", "parallel"),
        name="hgrn_step",
    )(p_all, p_all, p_all, p_all, lb.reshape(1, hw), gn.reshape(1, hw), state)


def _rwkv_step_prep_kernel(pr_ref, pk_ref, pv_ref, pc_ref, sr_ref, sk_ref, sv_ref, sc_ref,
                           mr_ref, mk_ref, mv_ref, mc_ref, vec_ref, lora_ref,
                           r_ref, k_ref, v_ref, kk_ref, kka_ref, w_ref, g_ref, bonus_ref):
    def mix(x_ref, prev_ref, mu_ref):
        x = x_ref[...]
        return x + mu_ref[...] * (prev_ref[...] - x)

    r, k, v, kk, kka, lw, g, bonus = _rw_prep(
        mix(pr_ref, sr_ref, mr_ref), mix(pk_ref, sk_ref, mk_ref), mix(pv_ref, sv_ref, mv_ref),
        mix(pc_ref, sc_ref, mc_ref), vec_ref[...], lora_ref[...], _head_ones())
    r_ref[...] = r
    k_ref[...] = k
    v_ref[...] = v
    kk_ref[...] = kk
    kka_ref[...] = kka
    w_ref[...] = jnp.exp(lw)
    g_ref[...] = g
    bonus_ref[...] = bonus


def _rwkv_step_prep(p_all, shift, mu, vec, lora, *, row0, col0):
    n, shift_w = shift.shape
    rw_w = RW_HEADS * RW_DIM
    nb = rw_w // LANES
    rb0 = row0 // n
    code_blk = (col0 + 3 * nb) // 2

    def pspec(seg):
        return pl.BlockSpec((n, LANES), lambda h: (rb0, col0 + seg * nb + h))

    def sspec(seg):
        return pl.BlockSpec((n, LANES), lambda h: (0, seg * nb + h))

    def mspec(seg):
        return pl.BlockSpec((1, LANES), lambda h: (0, seg * nb + h))

    ospec = pl.BlockSpec((n, LANES), lambda h: (0, h))
    mu2 = mu.reshape(1, shift_w)
    return pl.pallas_call(
        _rwkv_step_prep_kernel,
        out_shape=tuple(jax.ShapeDtypeStruct((n, rw_w), F32) for _ in range(8)),
        grid=(nb,),
        in_specs=[pspec(0), pspec(1), pspec(2), pl.BlockSpec((n, 2 * LANES), lambda h: (rb0, code_blk)),
                  sspec(0), sspec(1), sspec(2), pl.BlockSpec((n, 2 * LANES), lambda h: (0, 3 * nb // 2)),
                  mspec(0), mspec(1), mspec(2), pl.BlockSpec((1, 2 * LANES), lambda h: (0, 3 * nb // 2)),
                  pl.BlockSpec((8, LANES), lambda h: (0, h)), pl.BlockSpec((3, LANES, LANES), lambda h: (0, 0, h))],
        out_specs=tuple(ospec for _ in range(8)),
        compiler_params=_params("parallel"),
        name="rwkv_step_prep",
    )(p_all, p_all, p_all, p_all, shift, shift, shift, shift, mu2, mu2, mu2, mu2, vec, lora)


def _rwkv_step_kernel(s_ref, r_ref, k_ref, kk_ref, kka_ref, w_ref, vt_ref, so_ref, yt_ref):
    def head(h, carry):
        rows = pl.ds(pl.multiple_of(h * RW_DIM, RW_DIM), RW_DIM)
        for j in range(STEP_REQ):
            s = s_ref[j, h]
            sa = jnp.sum(s * kk_ref[j, pl.ds(h, 1), :], axis=1, keepdims=True)
            s_new = (s * w_ref[j, pl.ds(h, 1), :] - sa * kka_ref[j, pl.ds(h, 1), :]
                     + vt_ref[0, rows, j:j + 1] * k_ref[j, pl.ds(h, 1), :])
            so_ref[j, h] = s_new
            yt_ref[0, rows, j:j + 1] = jnp.sum(s_new * r_ref[j, pl.ds(h, 1), :], axis=1, keepdims=True)
        return carry

    lax.fori_loop(0, RW_HEADS, head, 0)


def _rwkv_step(state, r, k, kk, kka, w, v):
    n = state.shape[0]
    rw_w = RW_HEADS * RW_DIM
    nblk = n // STEP_REQ
    heads = lambda t: t.reshape(n, RW_HEADS, RW_DIM)
    v_t = v.reshape(nblk, STEP_REQ, rw_w).transpose(0, 2, 1)
    sspec = pl.BlockSpec((STEP_REQ, RW_HEADS, RW_DIM, RW_DIM), lambda i: (i, 0, 0, 0))
    rspec = pl.BlockSpec((STEP_REQ, RW_HEADS, RW_DIM), lambda i: (i, 0, 0))
    cspec = pl.BlockSpec((1, rw_w, STEP_REQ), lambda i: (i, 0, 0))
    s_new, y_t = pl.pallas_call(
        _rwkv_step_kernel,
        out_shape=(jax.ShapeDtypeStruct(state.shape, F32), jax.ShapeDtypeStruct((nblk, rw_w, STEP_REQ), F32)),
        grid=(nblk,),
        in_specs=[sspec, rspec, rspec, rspec, rspec, rspec, cspec],
        out_specs=(sspec, cspec),
        compiler_params=_params("parallel", vmem=VMEM_LIMIT_BYTES),
        name="rwkv_step",
    )(state, heads(r), heads(k), heads(kk), heads(kka), heads(w), v_t)
    return y_t.transpose(0, 2, 1).reshape(n, rw_w), s_new


def _rwkv_step_post_kernel(y_ref, bonus_ref, g_ref, vec_ref, o_ref):
    o_ref[...] = _rw_post(y_ref[...], bonus_ref[...], g_ref[...], vec_ref[...], _head_ones()).astype(o_ref.dtype)


def _rwkv_step_post(y, bonus, g, vec):
    n, rw_w = y.shape
    spec = pl.BlockSpec((n, LANES), lambda h: (0, h))
    return pl.pallas_call(
        _rwkv_step_post_kernel,
        out_shape=jax.ShapeDtypeStruct((n, rw_w), BF16),
        grid=(rw_w // LANES,),
        in_specs=[spec, spec, spec, pl.BlockSpec((8, LANES), lambda h: (0, h))],
        out_specs=spec,
        compiler_params=_params("parallel"),
        name="rwkv_step_post",
    )(y, bonus, g, vec)


def _merge_kernel(oh_ref, or_ref, wh_ref, wr_ref, gh_ref, gr_ref, o_ref, whb_ref, wrb_ref):
    @pl.when(pl.program_id(1) == 0)
    def _():
        whb_ref[...] = wh_ref[...].astype(BF16)
        wrb_ref[...] = wr_ref[...].astype(BF16)

    m = (jax.nn.sigmoid(gh_ref[...]) * _nn(oh_ref[...], whb_ref[...])
         + jax.nn.sigmoid(gr_ref[...]) * _nn(or_ref[...], wrb_ref[...]))
    o_ref[...] = m.astype(o_ref.dtype)


def _merge(o_hg, o_rw, w_hg, w_rw, p_all, *, gate_col0, tm, tn):
    m, kh = o_hg.shape
    kr = o_rw.shape[1]
    n = w_hg.shape[1]
    gb = gate_col0 // tn
    return pl.pallas_call(
        _merge_kernel,
        out_shape=jax.ShapeDtypeStruct((m, n), BF16),
        grid=(n // tn, m // tm),
        in_specs=[pl.BlockSpec((tm, kh), lambda j, i: (i, 0)), pl.BlockSpec((tm, kr), lambda j, i: (i, 0)),
                  pl.BlockSpec((kh, tn), lambda j, i: (0, j)), pl.BlockSpec((kr, tn), lambda j, i: (0, j)),
                  pl.BlockSpec((tm, tn), lambda j, i: (i, gb + j)),
                  pl.BlockSpec((tm, tn), lambda j, i: (i, gb + n // tn + j))],
        out_specs=pl.BlockSpec((tm, tn), lambda j, i: (i, j)),
        scratch_shapes=[pltpu.VMEM((kh, tn), BF16), pltpu.VMEM((kr, tn), BF16)],
        compiler_params=_params("parallel", "arbitrary"),
        name="merge",
    )(o_hg, o_rw, w_hg, w_rw, p_all, p_all)


def _softmax_rows(s):
    e = jnp.exp(s - jnp.max(s, axis=-1, keepdims=True))
    return e / jnp.sum(e, axis=-1, keepdims=True)


def _attn_heads(q, kb_ref, vb_ref, o_ref, store):
    dh = q.shape[-1] // CA_HEADS
    scale = dh ** -0.5
    for h in range(CA_HEADS):
        hs = slice(h * dh, (h + 1) * dh)
        p = _softmax_rows(_nt(q[:, hs], kb_ref[:, hs]) * scale)
        store(hs, _nn(p.astype(BF16), vb_ref[:, hs]))


def _attn_prompt_kernel(q_ref, k_ref, v_ref, o_ref, kb_ref, vb_ref):
    @pl.when(pl.program_id(1) == 0)
    def _():
        kb_ref[...] = k_ref[0].astype(BF16)
        vb_ref[...] = v_ref[0].astype(BF16)

    def store(hs, o):
        o_ref[:, hs] = o.astype(o_ref.dtype)

    _attn_heads(q_ref[...], kb_ref, vb_ref, o_ref, store)


def _attn_prompt(q_all, mem_k, mem_v, *, batch, seq, tq):
    _, mt, d = mem_k.shape
    nl = seq // tq
    kv = pl.BlockSpec((1, mt, d), lambda b, l: (b, 0, 0))
    qs = pl.BlockSpec((tq, d), lambda b, l: (b * nl + l, 0))
    return pl.pallas_call(
        _attn_prompt_kernel,
        out_shape=jax.ShapeDtypeStruct((batch * seq, d), BF16),
        grid=(batch, nl),
        in_specs=[qs, kv, kv],
        out_specs=qs,
        scratch_shapes=[pltpu.VMEM((mt, d), BF16), pltpu.VMEM((mt, d), BF16)],
        compiler_params=_params("parallel", "arbitrary"),
        name="attn_prompt",
    )(q_all, mem_k, mem_v)


def _attn_step_kernel(q_ref, k_ref, v_ref, o_ref):
    dh = k_ref.shape[-1]
    scale = dh ** -0.5
    q = jnp.broadcast_to(q_ref[0], (8, q_ref.shape[-1]))
    for h in range(k_ref.shape[2]):
        hs = slice(h * dh, (h + 1) * dh)
        p = _softmax_rows(_nt(q[:, hs], k_ref[0, :, h, :].astype(BF16)) * scale)
        o = _nn(p.astype(BF16), v_ref[0, :, h, :].astype(BF16))
        o_ref[0, :, hs] = o[0:1].astype(o_ref.dtype)


def _attn_step(q, mem_k, mem_v):
    n, mt, nh, dh = mem_k.shape
    d = nh * dh
    kv = pl.BlockSpec((1, mt, nh, dh), lambda b: (b, 0, 0, 0))
    qs = pl.BlockSpec((1, 1, d), lambda b: (b, 0, 0))
    o = pl.pallas_call(
        _attn_step_kernel,
        out_shape=jax.ShapeDtypeStruct((n, 1, d), BF16),
        grid=(n,),
        in_specs=[qs, kv, kv],
        out_specs=qs,
        compiler_params=_params("parallel"),
        name="attn_step",
    )(q.reshape(n, 1, d), mem_k, mem_v)
    return o.reshape(n, d)


_RT_E1, _RT_E2, _RT_C1, _RT_C2 = range(4)


def _router_kernel(x_ref, g_ref, wr_ref, br_ref, h_ref, rt_ref):
    x = x_ref[...]
    h = x * lax.rsqrt(jnp.mean(x * x, axis=-1, keepdims=True) + RMS_EPS) * g_ref[...]
    h_ref[...] = h
    logits = _dot3(h, wr_ref[...]) + br_ref[...]
    lane = _iota2(logits.shape, 1)
    lane_f = lane.astype(F32)
    neg = -jnp.inf
    first = lambda hit: jnp.min(jnp.where(hit, lane_f, float(LANES)), axis=-1, keepdims=True)

    is_group = lane < MOE_GROUPS
    gl = jnp.where(is_group, logits, neg)
    g_max = jnp.max(gl, axis=-1, keepdims=True)
    g_top = first(gl == g_max)
    g_w = 1.0 / jnp.sum(jnp.where(is_group, jnp.exp(logits - g_max), 0.0), axis=-1, keepdims=True)

    e_idx = lane - MOE_GROUPS
    in_group = (e_idx >= 0) & (e_idx < MOE_EXPERTS) & (jnp.right_shift(e_idx, MOE_PER_GROUP.bit_length() - 1).astype(F32) == g_top)
    el = jnp.where(in_group, logits, neg)
    m1 = jnp.max(el, axis=-1, keepdims=True)
    i1 = first(el == m1)
    el2 = jnp.where(lane_f == i1, neg, el)
    m2 = jnp.max(el2, axis=-1, keepdims=True)
    i2 = first(el2 == m2)
    t = jnp.exp(m2 - m1)
    w1 = 1.0 / (1.0 + t)
    w2 = t / (1.0 + t)
    rec = jnp.where(lane == _RT_E1, i1 - MOE_GROUPS, 0.0)
    rec = jnp.where(lane == _RT_E2, i2 - MOE_GROUPS, rec)
    rec = jnp.where(lane == _RT_C1, g_w * w1, rec)
    rec = jnp.where(lane == _RT_C2, g_w * w2, rec)
    rt_ref[...] = rec


def _router(x, g, w_rg, b_rg, w_re, b_re, *, tm):
    t, d = x.shape
    pad = LANES - MOE_GROUPS - MOE_EXPERTS
    wr = jnp.concatenate([w_rg, w_re, jnp.zeros((d, pad), F32)], axis=1)
    br = jnp.concatenate([b_rg, b_re, jnp.zeros((pad,), F32)]).reshape(1, LANES)
    return pl.pallas_call(
        _router_kernel,
        out_shape=(jax.ShapeDtypeStruct((t, d), F32), jax.ShapeDtypeStruct((t, LANES), F32)),
        grid=(t // tm,),
        in_specs=[pl.BlockSpec((tm, d), lambda i: (i, 0)), pl.BlockSpec((1, d), lambda i: (0, 0)),
                  pl.BlockSpec((d, LANES), lambda i: (0, 0)), pl.BlockSpec((1, LANES), lambda i: (0, 0))],
        out_specs=(pl.BlockSpec((tm, d), lambda i: (i, 0)), pl.BlockSpec((tm, LANES), lambda i: (i, 0))),
        compiler_params=_params("parallel", vmem=VMEM_LIMIT_BYTES),
        name="moe_router",
    )(x, g.reshape(1, d), wr, br)


def _dispatch_plan(e1, e2, n_tiles):
    t = e1.shape[0]
    keys = jnp.concatenate([e1, e2])
    onehot = (keys[:, None] == jnp.arange(MOE_EXPERTS, dtype=jnp.int32)[None, :]).astype(jnp.int32)
    rank = jnp.take_along_axis(jnp.cumsum(onehot, axis=0), keys[:, None], axis=1)[:, 0] - 1
    counts = jnp.sum(onehot, axis=0)
    tiles = (counts + MOE_TILE - 1) // MOE_TILE
    tile_end = jnp.cumsum(tiles)
    tile_start = tile_end - tiles
    dest = tile_start[keys] * MOE_TILE + rank
    token = jnp.concatenate([jnp.arange(t, dtype=jnp.int32)] * 2)
    row_src = jnp.zeros((n_tiles * MOE_TILE,), jnp.int32).at[dest].set(token)
    n_used = tile_end[-1].astype(jnp.int32)
    tile_id = jnp.arange(n_tiles, dtype=jnp.int32)
    tile_expert = jnp.sum(tile_id[:, None] >= tile_end[None, :], axis=1).astype(jnp.int32)
    tile_expert = jnp.minimum(tile_expert, tile_expert[jnp.maximum(n_used - 1, 0)])
    return row_src, tile_expert, n_used.reshape(1), dest.reshape(2, t).astype(jnp.int32)


def _expert_kernel(te_ref, nu_ref, src_ref, h_hbm, w1_ref, w3_ref, w2_ref, o_ref,
                   xbuf, sem, w1b, w3b, w2b):
    i = pl.program_id(0)
    n_used = nu_ref[0]

    def gather(tile, slot):
        def body(r, carry):
            row = src_ref[tile * MOE_TILE + r]
            pltpu.make_async_copy(h_hbm.at[pl.ds(row, 1)], xbuf.at[slot, pl.ds(r, 1)], sem.at[slot]).start()
            return carry
        lax.fori_loop(0, MOE_TILE, body, 0)

    @pl.when(i == 0)
    def _():
        gather(0, 0)

    @pl.when(i + 1 < n_used)
    def _():
        gather(i + 1, (i + 1) % 2)

    @pl.when(i < n_used)
    def _():
        slot = i % 2
        pltpu.make_async_copy(h_hbm.at[pl.ds(0, MOE_TILE)], xbuf.at[slot], sem.at[slot]).wait()

        @pl.when((i == 0) | (te_ref[i] != te_ref[jnp.maximum(i - 1, 0)]))
        def _():
            w1b[...] = w1_ref[0].astype(BF16)
            w3b[...] = w3_ref[0].astype(BF16)
            w2b[...] = w2_ref[0].astype(BF16)

        x = xbuf[slot].astype(BF16)
        hid = _silu(_nn(x, w1b[...])) * _nn(x, w3b[...])
        o_ref[...] = _nn(hid.astype(BF16), w2b[...])

    @pl.when(i >= n_used)
    def _():
        o_ref[...] = jnp.zeros_like(o_ref)


def _experts(h, w1, w3, w2, row_src, tile_expert, n_used):
    t, d = h.shape
    f = w1.shape[-1]
    n_tiles = tile_expert.shape[0]
    grid_spec = pltpu.PrefetchScalarGridSpec(
        num_scalar_prefetch=3,
        grid=(n_tiles,),
        in_specs=[pl.BlockSpec(memory_space=pl.ANY),
                  pl.BlockSpec((1, d, f), lambda i, te, nu, src: (te[i], 0, 0)),
                  pl.BlockSpec((1, d, f), lambda i, te, nu, src: (te[i], 0, 0)),
                  pl.BlockSpec((1, f, d), lambda i, te, nu, src: (te[i], 0, 0))],
        out_specs=pl.BlockSpec((MOE_TILE, d), lambda i, te, nu, src: (i, 0)),
        scratch_shapes=[pltpu.VMEM((2, MOE_TILE, d), F32), pltpu.SemaphoreType.DMA((2,)),
                        pltpu.VMEM((d, f), BF16), pltpu.VMEM((d, f), BF16), pltpu.VMEM((f, d), BF16)],
    )
    return pl.pallas_call(
        _expert_kernel,
        out_shape=jax.ShapeDtypeStruct((n_tiles * MOE_TILE, d), F32),
        grid_spec=grid_spec,
        compiler_params=_params("arbitrary", vmem=VMEM_LIMIT_BYTES),
        name="moe_experts",
    )(tile_expert, n_used, row_src, h, w1, w3, w2)


def _combine_kernel(pos_ref, x_ref, rt_ref, g_ref, ys_hbm, o_ref, ybuf, sem, *, tok0, n_tok):
    i = pl.program_id(0)
    tm = x_ref.shape[0]

    def gather(tile, slot):
        def body(r, carry):
            for s in range(2):
                row = pos_ref[s * n_tok + tok0 + tile * tm + r]
                pltpu.make_async_copy(ys_hbm.at[pl.ds(row, 1)], ybuf.at[slot, s, pl.ds(r, 1)], sem.at[slot]).start()
            return carry
        lax.fori_loop(0, tm, body, 0)

    @pl.when(i == 0)
    def _():
        gather(0, 0)

    @pl.when(i + 1 < pl.num_programs(0))
    def _():
        gather(i + 1, (i + 1) % 2)

    slot = i % 2
    for s in range(2):
        pltpu.make_async_copy(ys_hbm.at[pl.ds(0, tm)], ybuf.at[slot, s], sem.at[slot]).wait()
    rt = rt_ref[...]
    y = x_ref[...] + rt[:, _RT_C1:_RT_C1 + 1] * ybuf[slot, 0] + rt[:, _RT_C2:_RT_C2 + 1] * ybuf[slot, 1]
    o_ref[...] = y * lax.rsqrt(jnp.mean(y * y, axis=-1, keepdims=True) + RMS_EPS) * g_ref[...]


def _combine(x_all, rt, g, ys, pos, *, tok0, count, tm):
    n_tok, d = x_all.shape
    b0 = tok0 // tm
    grid_spec = pltpu.PrefetchScalarGridSpec(
        num_scalar_prefetch=1,
        grid=(count // tm,),
        in_specs=[pl.BlockSpec((tm, d), lambda i, pos: (b0 + i, 0)),
                  pl.BlockSpec((tm, LANES), lambda i, pos: (b0 + i, 0)),
                  pl.BlockSpec((1, d), lambda i, pos: (0, 0)),
                  pl.BlockSpec(memory_space=pl.ANY)],
        out_specs=pl.BlockSpec((tm, d), lambda i, pos: (i, 0)),
        scratch_shapes=[pltpu.VMEM((2, 2, tm, d), F32), pltpu.SemaphoreType.DMA((2,))],
    )
    return pl.pallas_call(
        functools.partial(_combine_kernel, tok0=tok0, n_tok=n_tok),
        out_shape=jax.ShapeDtypeStruct((count, d), F32),
        grid_spec=grid_spec,
        compiler_params=_params("arbitrary"),
        name="moe_combine",
    )(pos.reshape(-1), x_all, rt, g.reshape(1, d), ys)


def kernel(x_prompt, x_sample, mem_prompt, cache_mem_k, cache_mem_v, state_hgrn, state_rwkv, state_rwkv_shift, g_mix, w_in, hg_lb_logits, hg_norm_g, rw_mu, rw_w0, rw_w2, rw_a0, rw_a2, rw_g2, rw_kk, rw_ka, rw_rk, rw_ln_g, rw_ln_b, w_br_hg, w_br_rw, w_out, g_ca, g_mem, w_ca_q, w_ca_k, w_ca_v, w_ca_o, g_moe, w_rg, b_rg, w_re, b_re, w_e1, w_e3, w_e2, g_final):
    bp, seq, d = x_prompt.shape
    ns = x_sample.shape[0]
    tp = bp * seq
    t_all = tp + ns
    assert w_in.shape[0] == 1, "the final RMSNorm is fused into the last layer's MoE combine"
    layer = 0
    hg_w = HG_HEADS * HG_DIM
    rw_w = RW_HEADS * RW_DIM
    shift_w = state_rwkv_shift.shape[-1]
    rw_col = 4 * hg_w
    gate_col = rw_col + shift_w
    mem_t = mem_prompt.shape[1]
    tm = 640
    assert t_all % tm == 0

    lb = jnp.cumsum(jax.nn.softmax(hg_lb_logits.astype(F32), axis=0), axis=0)[layer]
    x_all = jnp.concatenate([x_prompt.reshape(tp, d), x_sample.reshape(ns, d)])

    h = _rmsnorm(x_all, g_mix[layer], BF16, tm)
    p_all = _matmul(h, w_in[layer], tm=tm, tn=1280, out_dtype=F32)
    gn = hg_norm_g[layer]
    o_hg_p, hg_p = _hgrn_prompt(p_all, lb, gn, batch=bp, seq=seq, col0=0, lblk=512)
    o_hg_s, hg_s = _hgrn_step(p_all, state_hgrn.reshape(state_hgrn.shape[1:]), lb, gn, row0=tp, col0=0)
    P = {'rw_w0': rw_w0[layer], 'rw_a0': rw_a0[layer], 'rw_kk': rw_kk[layer], 'rw_ka': rw_ka[layer],
         'rw_rk': rw_rk[layer], 'rw_ln_g': rw_ln_g[layer], 'rw_ln_b': rw_ln_b[layer],
         'rw_w2': rw_w2[layer], 'rw_a2': rw_a2[layer], 'rw_g2': rw_g2[layer]}
    vec, lora = _rw_param_blocks(P)
    mu = rw_mu[layer]
    o_rw_p, rw_p = _rwkv_prompt(p_all, jnp.zeros((bp, shift_w), F32), mu, vec, lora,
                                batch=bp, seq=seq, col0=rw_col // LANES, lblk=512)
    r_s, k_s, v_s, kk_s, kka_s, w_s, g_s, bonus_s = _rwkv_step_prep(
        p_all, state_rwkv_shift.reshape(ns, shift_w), mu, vec, lora, row0=tp, col0=rw_col // LANES)
    y_s, rw_s = _rwkv_step(state_rwkv.reshape(state_rwkv.shape[1:]), r_s, k_s, kk_s, kka_s, w_s, v_s)
    o_rw_s = _rwkv_step_post(y_s, bonus_s, g_s, vec)
    shift_p = p_all[seq - 1:tp:seq, rw_col:gate_col]
    shift_s = p_all[tp:, rw_col:gate_col]
    merged = _merge(jnp.concatenate([o_hg_p, o_hg_s]), jnp.concatenate([o_rw_p, o_rw_s]),
                    w_br_hg[layer], w_br_rw[layer], p_all, gate_col0=gate_col, tm=tm, tn=256)
    x1 = _matmul(merged, w_out[layer], tm=tm, tn=512, out_dtype=F32, res=x_all)

    q = _matmul(_rmsnorm(x1, g_ca[layer], BF16, tm), w_ca_q[layer], tm=tm, tn=512, out_dtype=BF16)
    m = _rmsnorm(mem_prompt.reshape(bp * mem_t, d), g_mem[layer], BF16, mem_t)
    mem_k = _matmul(m, w_ca_k[layer], tm=bp * mem_t, tn=512, out_dtype=F32)
    mem_v = _matmul(m, w_ca_v[layer], tm=bp * mem_t, tn=512, out_dtype=F32)
    o_p = _attn_prompt(q, mem_k.reshape(bp, mem_t, d), mem_v.reshape(bp, mem_t, d), batch=bp, seq=seq, tq=512)
    o_s = _attn_step(q[tp:], cache_mem_k.reshape(cache_mem_k.shape[1:]), cache_mem_v.reshape(cache_mem_v.shape[1:]))
    x2 = _matmul(jnp.concatenate([o_p, o_s]), w_ca_o[layer], tm=tm, tn=512, out_dtype=F32, res=x1)

    h3, rt = _router(x2, g_moe[layer], w_rg[layer], b_rg[layer], w_re[layer], b_re[layer], tm=tm)
    n_tiles = (2 * t_all + MOE_EXPERTS * (MOE_TILE - 1)) // MOE_TILE
    row_src, tile_expert, n_used, pos = _dispatch_plan(
        rt[:, _RT_E1].astype(jnp.int32), rt[:, _RT_E2].astype(jnp.int32), n_tiles)
    ys = _experts(h3, w_e1[layer], w_e3[layer], w_e2[layer], row_src, tile_expert, n_used)
    y_p = _combine(x2, rt, g_final, ys, pos, tok0=0, count=tp, tm=LANES)
    y_s = _combine(x2, rt, g_final, ys, pos, tok0=tp, count=ns, tm=LANES)

    ca_heads_shape = (1, bp, mem_t, CA_HEADS, d // CA_HEADS)
    return (y_p.reshape(bp, seq, d), y_s.reshape(ns, 1, d), hg_p[None], rw_p[None], shift_p[None],
            mem_k.reshape(ca_heads_shape), mem_v.reshape(ca_heads_shape), hg_s[None], rw_s[None], shift_s[None])
```

```python
import functools

import jax
import jax.numpy as jnp
from jax import lax
from jax.experimental import pallas as pl
from jax.experimental.pallas import tpu as pltpu

F32 = jnp.float32
BF16 = jnp.bfloat16

RMS_EPS = 1e-6
GN_EPS = 64e-5

LANES = 128
VMEM_LIMIT_BYTES = 56 * 1024 * 1024

HG_HEADS = 8
HG_DIM = 128
HG_CHUNK = 64
RW_HEADS = 16
RW_DIM = 64
RW_CHUNK = 64
RW_SUB = 16
CA_HEADS = 4
MOE_GROUPS = 4
MOE_PER_GROUP = 8
MOE_EXPERTS = MOE_GROUPS * MOE_PER_GROUP
MOE_TILE = 256


def _params(*sem, vmem=None):
    return pltpu.CompilerParams(dimension_semantics=sem, vmem_limit_bytes=vmem)


def _nt(a, b):
    return lax.dot_general(a, b, (((1,), (1,)), ((), ())), preferred_element_type=F32)


def _tn(a, b):
    return lax.dot_general(a, b, (((0,), (0,)), ((), ())), preferred_element_type=F32)


def _nn(a, b):
    return jnp.dot(a, b, preferred_element_type=F32)


def _split2(x):
    hi = x.astype(BF16)
    lo = (x - hi.astype(F32)).astype(BF16)
    return hi, lo


def _split3(x):
    hi = x.astype(BF16)
    r = x - hi.astype(F32)
    mid = r.astype(BF16)
    lo = (r - mid.astype(F32)).astype(BF16)
    return hi, mid, lo


def _dot3(a, b, dot=_nn):
    ah, al = _split2(a)
    bh, bl = _split2(b)
    return dot(ah, bh) + (dot(ah, bl) + dot(al, bh))


def _dot1(a, b):
    return _nn(a.astype(BF16), b.astype(BF16))


def _dot_exact_rhs(a, b_bf16, dot=_nn):
    h, m, l = _split3(a)
    return dot(h, b_bf16) + (dot(m, b_bf16) + dot(l, b_bf16))


def _iota2(shape, dim):
    return lax.broadcasted_iota(jnp.int32, shape, dim)


def _silu(x):
    return x * jax.nn.sigmoid(x)


def _rms_kernel(x_ref, g_ref, o_ref):
    x = x_ref[...]
    ms = jnp.mean(x * x, axis=-1, keepdims=True)
    o_ref[...] = (x * lax.rsqrt(ms + RMS_EPS) * g_ref[...]).astype(o_ref.dtype)


def _rmsnorm(x, g, out_dtype, tm):
    t, d = x.shape
    return pl.pallas_call(
        _rms_kernel,
        out_shape=jax.ShapeDtypeStruct((t, d), out_dtype),
        grid=(t // tm,),
        in_specs=[pl.BlockSpec((tm, d), lambda i: (i, 0)), pl.BlockSpec((1, d), lambda i: (0, 0))],
        out_specs=pl.BlockSpec((tm, d), lambda i: (i, 0)),
        compiler_params=_params("parallel"),
        name="rmsnorm",
    )(x, g.reshape(1, d))


def _mm_kernel(x_ref, w_ref, *rest, has_res):
    if has_res:
        r_ref, o_ref, wb_ref = rest
    else:
        o_ref, wb_ref = rest

    @pl.when(pl.program_id(1) == 0)
    def _():
        wb_ref[...] = w_ref[...].astype(BF16)

    acc = _nn(x_ref[...], wb_ref[...])
    if has_res:
        acc = r_ref[...] + acc
    o_ref[...] = acc.astype(o_ref.dtype)


def _matmul(x, w, *, tm, tn, out_dtype, res=None):
    m, k = x.shape
    n = w.shape[1]
    in_specs = [pl.BlockSpec((tm, k), lambda j, i: (i, 0)), pl.BlockSpec((k, tn), lambda j, i: (0, j))]
    args = [x, w]
    if res is not None:
        in_specs.append(pl.BlockSpec((tm, tn), lambda j, i: (i, j)))
        args.append(res)
    return pl.pallas_call(
        functools.partial(_mm_kernel, has_res=res is not None),
        out_shape=jax.ShapeDtypeStruct((m, n), out_dtype),
        grid=(n // tn, m // tm),
        in_specs=in_specs,
        out_specs=pl.BlockSpec((tm, tn), lambda j, i: (i, j)),
        scratch_shapes=[pltpu.VMEM((k, tn), BF16)],
        compiler_params=_params("parallel", "arbitrary", vmem=VMEM_LIMIT_BYTES),
        name="matmul",
    )(*args)


def _hgrn_gates(hq, hf, lb):
    q = _silu(hq)
    f = lb + (1.0 - lb) * jax.nn.sigmoid(hf)
    return q, f


def _hgrn_out(o, gn, hog):
    ms = jnp.mean(o * o, axis=-1, keepdims=True)
    return o * lax.rsqrt(ms + RMS_EPS) * gn * _silu(hog)


def _hgrn_prompt_kernel(q_ref, f_ref, i_ref, og_ref, lb_ref, gn_ref, o_ref, s_ref, st_ref, *, n_chunks):
    l = pl.program_id(2)
    c = HG_CHUNK

    @pl.when(l == 0)
    def _():
        st_ref[...] = jnp.zeros_like(st_ref)

    lb = lb_ref[...]
    gn = gn_ref[...]
    incl = _iota2((c, c), 0) >= _iota2((c, c), 1)
    tri = incl.astype(BF16)
    for ci in range(n_chunks):
        sl = slice(ci * c, (ci + 1) * c)
        q, f = _hgrn_gates(q_ref[sl, :], f_ref[sl, :], lb)
        k = 1.0 - f
        vb = i_ref[sl, :].astype(BF16)
        b = _cumsum_rows(jnp.log(f), tri)
        b_last = b[c - 1:c, :]
        q_in = (q * jnp.exp(b)).astype(BF16)
        k_in = (k * jnp.exp(-b)).astype(BF16)
        k_out = (k * jnp.exp(b_last - b)).astype(BF16)
        st = st_ref[...]
        o_inter = _nt(q_in, st.astype(BF16))
        att = jnp.where(incl, _nt(q_in, k_in), 0.0)
        o_intra = _nn(att.astype(BF16), vb)
        st_ref[...] = st * jnp.exp(b_last) + _tn(vb, k_out)
        o_ref[sl, :] = _hgrn_out(o_inter + o_intra, gn, og_ref[sl, :]).astype(o_ref.dtype)

    @pl.when(l == pl.num_programs(2) - 1)
    def _():
        s_ref[0, 0] = st_ref[...].T


def _cumsum_rows(x, tri_bf16):
    h, m, l = _split3(x)
    return _nn(tri_bf16, h) + (_nn(tri_bf16, m) + _nn(tri_bf16, l))


def _hgrn_prompt(p_all, lb, gn, *, batch, seq, col0, lblk):
    nl = seq // lblk
    hw = HG_HEADS * HG_DIM
    nb = hw // LANES

    def pspec(seg):
        return pl.BlockSpec((lblk, LANES), lambda b, h, l: (b * nl + l, col0 + seg * nb + h))

    vec = pl.BlockSpec((1, LANES), lambda b, h, l: (0, h))
    return pl.pallas_call(
        functools.partial(_hgrn_prompt_kernel, n_chunks=lblk // HG_CHUNK),
        out_shape=(jax.ShapeDtypeStruct((batch * seq, hw), BF16),
                   jax.ShapeDtypeStruct((batch, HG_HEADS, HG_DIM, HG_DIM), F32)),
        grid=(batch, HG_HEADS, nl),
        in_specs=[pspec(0), pspec(1), pspec(2), pspec(3), vec, vec],
        out_specs=(pl.BlockSpec((lblk, LANES), lambda b, h, l: (b * nl + l, h)),
                   pl.BlockSpec((1, 1, HG_DIM, HG_DIM), lambda b, h, l: (b, h, 0, 0))),
        scratch_shapes=[pltpu.VMEM((HG_DIM, HG_DIM), F32)],
        compiler_params=_params("parallel", "parallel", "arbitrary"),
        name="hgrn_prompt",
    )(p_all, p_all, p_all, p_all, lb.reshape(1, hw), gn.reshape(1, hw))


_RW_W0, _RW_A0, _RW_KK, _RW_KA, _RW_RK, _RW_LNG, _RW_LNB = range(7)


def _head_ones():
    same = (_iota2((LANES, LANES), 0) // RW_DIM) == (_iota2((LANES, LANES), 1) // RW_DIM)
    return same.astype(BF16)


def _head_sum(x, ones_bd):
    return _dot_exact_rhs(x, ones_bd)


def _rw_prep(xr, xk, xv, xc, vec, lora, ones_bd):
    c1 = xc[:, :LANES]
    w_lora = _nn(jnp.tanh(c1).astype(BF16), lora[0].astype(BF16))
    a_lora = _nn(c1.astype(BF16), lora[1].astype(BF16))
    g = _nn(jax.nn.sigmoid(xc[:, LANES:]).astype(BF16), lora[2].astype(BF16))
    z = -(vec[_RW_W0:_RW_W0 + 1] + w_lora)
    softplus = jnp.maximum(z, 0.0) + jnp.log1p(jnp.exp(-jnp.abs(z)))
    lw = -jnp.exp(-softplus - 0.5)
    a = jax.nn.sigmoid(vec[_RW_A0:_RW_A0 + 1] + a_lora)
    kk = xk * vec[_RW_KK:_RW_KK + 1]
    kk = kk / jnp.maximum(jnp.sqrt(_head_sum(kk * kk, ones_bd)), 1e-12)
    k = xk * (1.0 + (a - 1.0) * vec[_RW_KA:_RW_KA + 1])
    bonus = _head_sum(xr * k * vec[_RW_RK:_RW_RK + 1], ones_bd) * xv
    return xr, k, xv, kk, kk * a, lw, g, bonus


def _rw_post(y, bonus, g, vec, ones_bd):
    inv = 1.0 / RW_DIM
    mu = _head_sum(y, ones_bd) * inv
    d = y - mu
    var = _head_sum(d * d, ones_bd) * inv
    yn = d * lax.rsqrt(var + GN_EPS) * vec[_RW_LNG:_RW_LNG + 1] + vec[_RW_LNB:_RW_LNB + 1]
    return (yn + bonus) * g


def _rw_chunk_operands(r, k, v, kk, kka, lw, tri):
    c = r.shape[0]
    ginc = _cumsum_rows(lw, tri)
    glast = ginc[c - 1:c, :]
    e_neg = jnp.exp(-ginc)
    e_rem = jnp.exp(glast - ginc)
    a_hat = kk * jnp.exp(ginc - lw)
    r_hat = r * jnp.exp(ginc)
    e_last = jnp.exp(glast)
    b_hat, k_hat, b_til, k_til = kka * e_neg, k * e_neg, kka * e_rem, k * e_rem
    units = []
    for h in range(LANES // RW_DIM):
        s = slice(h * RW_DIM, (h + 1) * RW_DIM)
        units.append(dict(
            ah=a_hat[:, s], rh=r_hat[:, s], ab=a_hat[:, s].astype(BF16), rb=r_hat[:, s].astype(BF16),
            bb=b_hat[:, s].astype(BF16), kb=k_hat[:, s].astype(BF16), bt=b_til[:, s].astype(BF16),
            kt=k_til[:, s].astype(BF16), vb=v[:, s].astype(BF16), e_last=e_last[:, s]))
    return units


def _rw_local_levels(us, out):
    c = us[0]['ab'].shape[0]
    row, col = _iota2((c, c), 0), _iota2((c, c), 1)
    incl, strict = row >= col, row > col
    diag_blk = (row // RW_SUB) == (col // RW_SUB)
    eye_c = (row == col).astype(F32)
    eye_d = _iota2((RW_DIM, RW_DIM), 0) == _iota2((RW_DIM, RW_DIM), 1)

    lm = [jnp.where(strict, _nt(u['ab'], u['bb']), 0.0) for u in us]
    lk = [jnp.where(strict, _nt(u['ab'], u['kb']), 0.0).astype(BF16) for u in us]
    pb = [jnp.where(incl, _nt(u['rb'], u['bb']), 0.0).astype(BF16) for u in us]
    pk = [jnp.where(incl, _nt(u['rb'], u['kb']), 0.0).astype(BF16) for u in us]
    yield
    pw = [jnp.where(diag_blk, -x, 0.0) for x in lm]
    e = [jnp.where(diag_blk, 0.0, x) for x in lm]
    t_d = [eye_c + x for x in pw]
    x = [jnp.concatenate([u['ah'], _nn(lk_, u['vb'])], axis=1) for u, lk_ in zip(us, lk)]
    for _ in range(RW_SUB.bit_length() - 2):
        pw = [_dot1(p, p) for p in pw]
        yield
        t_d = [t + _dot1(t, p) for t, p in zip(t_d, pw)]
    yield
    m = [_dot1(t, e_) for t, e_ in zip(t_d, e)]
    tdx = [_dot1(t, x_) for t, x_ in zip(t_d, x)]
    yield
    acc = [eye_c - m_ for m_ in m]
    pw = m
    for _ in range((c // RW_SUB).bit_length() - 2):
        pw = [_dot1(p, p) for p in pw]
        yield
        acc = [a + _dot1(a, p) for a, p in zip(acc, pw)]
        yield
    txb = [_dot1(a, t).astype(BF16) for a, t in zip(acc, tdx)]
    yield
    p_au = [_nn(p, t) for p, t in zip(pb, txb)]
    b_au = [_tn(u['bt'], t) for u, t in zip(us, txb)]
    pkv = [_nn(p, u['vb']) for u, p in zip(us, pk)]
    ktv = [_tn(u['kt'], u['vb']) for u in us]
    yield
    for i, u in enumerate(us):
        out.append((u['rh'] - p_au[i][:, :RW_DIM], pkv[i] - p_au[i][:, RW_DIM:],
                    jnp.where(eye_d, u['e_last'], 0.0) - b_au[i][:, :RW_DIM], ktv[i] - b_au[i][:, RW_DIM:]))


RW_GROUP_CHUNKS = 4


def _rwkv_prompt_kernel(pr_ref, pk_ref, pv_ref, pc_ref, sr_ref, sk_ref, sv_ref, sc_ref,
                        mr_ref, mk_ref, mv_ref, mc_ref, vec_ref, lora_ref,
                        o_ref, s_ref, a_ref, y_ref, cr_ref, ck_ref, cv_ref, cc_ref, *, n_chunks):
    l = pl.program_id(2)
    lblk = pr_ref.shape[0]
    c = RW_CHUNK

    @pl.when(l == 0)
    def _():
        a_ref[...] = jnp.zeros_like(a_ref)
        cr_ref[...] = sr_ref[0]
        ck_ref[...] = sk_ref[0]
        cv_ref[...] = sv_ref[0]
        cc_ref[...] = sc_ref[0]

    def shift_mix(x_ref, carry_ref, mu_ref):
        x = x_ref[...]
        first = _iota2(x.shape, 0) == 0
        prev = jnp.where(first, carry_ref[...], pltpu.roll(x, 1, 0))
        carry_ref[...] = x[lblk - 1:lblk, :]
        return x + mu_ref[...] * (prev - x)

    ones_bd = _head_ones()
    vec = vec_ref[...]
    r, k, v, kk, kka, lw, g, bonus = _rw_prep(
        shift_mix(pr_ref, cr_ref, mr_ref), shift_mix(pk_ref, ck_ref, mk_ref), shift_mix(pv_ref, cv_ref, mv_ref),
        shift_mix(pc_ref, cc_ref, mc_ref), vec, lora_ref[...], ones_bd)
    tri = (_iota2((c, c), 0) >= _iota2((c, c), 1)).astype(BF16)
    n_heads = LANES // RW_DIM
    state = [a_ref[h] for h in range(n_heads)]

    def state_step(ci, local):
        ys = []
        for h in range(n_heads):
            r_til, y_loc, g_mat, a_loc = local[h]
            ys.append(_dot3(r_til, state[h]) + y_loc)
            state[h] = _dot3(g_mat, state[h]) + a_loc
        y_ref[ci * c:(ci + 1) * c, :] = jnp.concatenate(ys, axis=1)

    pending = []
    for g0 in range(0, n_chunks, RW_GROUP_CHUNKS):
        chunks = range(g0, min(g0 + RW_GROUP_CHUNKS, n_chunks))
        units = []
        for ci in chunks:
            sl = slice(ci * c, (ci + 1) * c)
            units += _rw_chunk_operands(r[sl], k[sl], v[sl], kk[sl], kka[sl], lw[sl], tri)
        local = []
        for _ in _rw_local_levels(units, local):
            if pending:
                pending.pop(0)()
        pending += [functools.partial(state_step, ci, local[i * n_heads:(i + 1) * n_heads])
                    for i, ci in enumerate(chunks)]
    for step in pending:
        step()
    for h in range(n_heads):
        a_ref[h] = state[h]
    o_ref[...] = _rw_post(y_ref[...], bonus, g, vec, ones_bd).astype(o_ref.dtype)

    @pl.when(l == pl.num_programs(2) - 1)
    def _():
        for h in range(n_heads):
            s_ref[0, h] = state[h].T


def _rw_param_blocks(P):
    rw_w = RW_HEADS * RW_DIM
    vec = jnp.stack([P['rw_w0'], P['rw_a0'], P['rw_kk'], P['rw_ka'], P['rw_rk'], P['rw_ln_g'], P['rw_ln_b'],
                     jnp.zeros((rw_w,), F32)])
    z = jnp.zeros_like(P['rw_w2'])
    lora = jnp.stack([jnp.concatenate([P['rw_w2'], z]), jnp.concatenate([z, P['rw_a2']]), P['rw_g2']])
    return vec, lora


def _rwkv_prompt(p_all, shift0, mu, vec, lora, *, batch, seq, col0, lblk):
    nl = seq // lblk
    rw_w = RW_HEADS * RW_DIM
    nb = rw_w // LANES
    code_blk = (col0 + 3 * nb) // 2
    shift_w = shift0.shape[-1]

    def pspec(seg):
        return pl.BlockSpec((lblk, LANES), lambda b, h, l: (b * nl + l, col0 + seg * nb + h))

    def sspec(seg):
        return pl.BlockSpec((1, 1, LANES), lambda b, h, l: (b, 0, seg * nb + h))

    def mspec(seg):
        return pl.BlockSpec((1, LANES), lambda b, h, l: (0, seg * nb + h))

    in_specs = [pspec(0), pspec(1), pspec(2),
                pl.BlockSpec((lblk, 2 * LANES), lambda b, h, l: (b * nl + l, code_blk)),
                sspec(0), sspec(1), sspec(2),
                pl.BlockSpec((1, 1, 2 * LANES), lambda b, h, l: (b, 0, 3 * nb // 2)),
                mspec(0), mspec(1), mspec(2),
                pl.BlockSpec((1, 2 * LANES), lambda b, h, l: (0, 3 * nb // 2)),
                pl.BlockSpec((8, LANES), lambda b, h, l: (0, h)),
                pl.BlockSpec((3, LANES, LANES), lambda b, h, l: (0, 0, h))]
    sh3 = shift0.reshape(batch, 1, shift_w)
    mu2 = mu.reshape(1, shift_w)
    return pl.pallas_call(
        functools.partial(_rwkv_prompt_kernel, n_chunks=lblk // RW_CHUNK),
        out_shape=(jax.ShapeDtypeStruct((batch * seq, rw_w), BF16),
                   jax.ShapeDtypeStruct((batch, RW_HEADS, RW_DIM, RW_DIM), F32)),
        grid=(batch, nb, nl),
        in_specs=in_specs,
        out_specs=(pl.BlockSpec((lblk, LANES), lambda b, h, l: (b * nl + l, h)),
                   pl.BlockSpec((1, 2, RW_DIM, RW_DIM), lambda b, h, l: (b, h, 0, 0))),
        scratch_shapes=[pltpu.VMEM((2, RW_DIM, RW_DIM), F32), pltpu.VMEM((lblk, LANES), F32),
                        pltpu.VMEM((1, LANES), F32), pltpu.VMEM((1, LANES), F32), pltpu.VMEM((1, LANES), F32),
                        pltpu.VMEM((1, 2 * LANES), F32)],
        compiler_params=_params("parallel", "parallel", "arbitrary"),
        name="rwkv_prompt",
    )(p_all, p_all, p_all, p_all, sh3, sh3, sh3, sh3, mu2, mu2, mu2, mu2, vec, lora)


STEP_REQ = 16


def _transpose_rows(x):
    r = x.shape[0]
    eye = (_iota2((r, r), 0) == _iota2((r, r), 1)).astype(BF16)
    return _dot_exact_rhs(x, eye, dot=_tn)


def _hgrn_step_kernel(q_ref, f_ref, i_ref, og_ref, lb_ref, gn_ref, s_ref, o_ref, so_ref):
    q, f = _hgrn_gates(q_ref[...], f_ref[...], lb_ref[...])
    f_t = _transpose_rows(f)
    q_t = _transpose_rows(q)
    v = i_ref[...]
    rows = []
    for j in range(STEP_REQ):
        fc = f_t[:, j:j + 1]
        s_new = s_ref[j, 0] * fc + (1.0 - fc) * v[j:j + 1, :]
        so_ref[j, 0] = s_new
        rows.append(jnp.sum(s_new * q_t[:, j:j + 1], axis=0, keepdims=True))
    o = jnp.concatenate(rows, axis=0)
    o_ref[...] = _hgrn_out(o, gn_ref[...], og_ref[...]).astype(o_ref.dtype)


def _hgrn_step(p_all, state, lb, gn, *, row0, col0):
    n = state.shape[0]
    hw = HG_HEADS * HG_DIM
    nb = hw // LANES
    rb0 = row0 // STEP_REQ

    def pspec(seg):
        return pl.BlockSpec((STEP_REQ, LANES), lambda i, h: (rb0 + i, col0 + seg * nb + h))

    vec = pl.BlockSpec((1, LANES), lambda i, h: (0, h))
    sspec = pl.BlockSpec((STEP_REQ, 1, HG_DIM, HG_DIM), lambda i, h: (i, h, 0, 0))
    return pl.pallas_call(
        _hgrn_step_kernel,
        out_shape=(jax.ShapeDtypeStruct((n, hw), BF16), jax.ShapeDtypeStruct(state.shape, F32)),
        grid=(n // STEP_REQ, HG_HEADS),
        in_specs=[pspec(0), pspec(1), pspec(2), pspec(3), vec, vec, sspec],
        out_specs=(pl.BlockSpec((STEP_REQ, LANES), lambda i, h: (i, h)), sspec),
        compiler_params=_params("parallel", "parallel"),
        name="hgrn_step",
    )(p_all, p_all, p_all, p_all, lb.reshape(1, hw), gn.reshape(1, hw), state)


def _rwkv_step_prep_kernel(pr_ref, pk_ref, pv_ref, pc_ref, sr_ref, sk_ref, sv_ref, sc_ref,
                           mr_ref, mk_ref, mv_ref, mc_ref, vec_ref, lora_ref,
                           r_ref, k_ref, v_ref, kk_ref, kka_ref, w_ref, g_ref, bonus_ref):
    def mix(x_ref, prev_ref, mu_ref):
        x = x_ref[...]
        return x + mu_ref[...] * (prev_ref[...] - x)

    r, k, v, kk, kka, lw, g, bonus = _rw_prep(
        mix(pr_ref, sr_ref, mr_ref), mix(pk_ref, sk_ref, mk_ref), mix(pv_ref, sv_ref, mv_ref),
        mix(pc_ref, sc_ref, mc_ref), vec_ref[...], lora_ref[...], _head_ones())
    r_ref[...] = r
    k_ref[...] = k
    v_ref[...] = v
    kk_ref[...] = kk
    kka_ref[...] = kka
    w_ref[...] = jnp.exp(lw)
    g_ref[...] = g
    bonus_ref[...] = bonus


def _rwkv_step_prep(p_all, shift, mu, vec, lora, *, row0, col0):
    n, shift_w = shift.shape
    rw_w = RW_HEADS * RW_DIM
    nb = rw_w // LANES
    rb0 = row0 // n
    code_blk = (col0 + 3 * nb) // 2

    def pspec(seg):
        return pl.BlockSpec((n, LANES), lambda h: (rb0, col0 + seg * nb + h))

    def sspec(seg):
        return pl.BlockSpec((n, LANES), lambda h: (0, seg * nb + h))

    def mspec(seg):
        return pl.BlockSpec((1, LANES), lambda h: (0, seg * nb + h))

    ospec = pl.BlockSpec((n, LANES), lambda h: (0, h))
    mu2 = mu.reshape(1, shift_w)
    return pl.pallas_call(
        _rwkv_step_prep_kernel,
        out_shape=tuple(jax.ShapeDtypeStruct((n, rw_w), F32) for _ in range(8)),
        grid=(nb,),
        in_specs=[pspec(0), pspec(1), pspec(2), pl.BlockSpec((n, 2 * LANES), lambda h: (rb0, code_blk)),
                  sspec(0), sspec(1), sspec(2), pl.BlockSpec((n, 2 * LANES), lambda h: (0, 3 * nb // 2)),
                  mspec(0), mspec(1), mspec(2), pl.BlockSpec((1, 2 * LANES), lambda h: (0, 3 * nb // 2)),
                  pl.BlockSpec((8, LANES), lambda h: (0, h)), pl.BlockSpec((3, LANES, LANES), lambda h: (0, 0, h))],
        out_specs=tuple(ospec for _ in range(8)),
        compiler_params=_params("parallel"),
        name="rwkv_step_prep",
    )(p_all, p_all, p_all, p_all, shift, shift, shift, shift, mu2, mu2, mu2, mu2, vec, lora)


def _rwkv_step_kernel(s_ref, r_ref, k_ref, kk_ref, kka_ref, w_ref, vt_ref, so_ref, yt_ref):
    def head(h, carry):
        rows = pl.ds(pl.multiple_of(h * RW_DIM, RW_DIM), RW_DIM)
        for j in range(STEP_REQ):
            s = s_ref[j, h]
            sa = jnp.sum(s * kk_ref[j, pl.ds(h, 1), :], axis=1, keepdims=True)
            s_new = (s * w_ref[j, pl.ds(h, 1), :] - sa * kka_ref[j, pl.ds(h, 1), :]
                     + vt_ref[0, rows, j:j + 1] * k_ref[j, pl.ds(h, 1), :])
            so_ref[j, h] = s_new
            yt_ref[0, rows, j:j + 1] = jnp.sum(s_new * r_ref[j, pl.ds(h, 1), :], axis=1, keepdims=True)
        return carry

    lax.fori_loop(0, RW_HEADS, head, 0)


def _rwkv_step(state, r, k, kk, kka, w, v):
    n = state.shape[0]
    rw_w = RW_HEADS * RW_DIM
    nblk = n // STEP_REQ
    heads = lambda t: t.reshape(n, RW_HEADS, RW_DIM)
    v_t = v.reshape(nblk, STEP_REQ, rw_w).transpose(0, 2, 1)
    sspec = pl.BlockSpec((STEP_REQ, RW_HEADS, RW_DIM, RW_DIM), lambda i: (i, 0, 0, 0))
    rspec = pl.BlockSpec((STEP_REQ, RW_HEADS, RW_DIM), lambda i: (i, 0, 0))
    cspec = pl.BlockSpec((1, rw_w, STEP_REQ), lambda i: (i, 0, 0))
    s_new, y_t = pl.pallas_call(
        _rwkv_step_kernel,
        out_shape=(jax.ShapeDtypeStruct(state.shape, F32), jax.ShapeDtypeStruct((nblk, rw_w, STEP_REQ), F32)),
        grid=(nblk,),
        in_specs=[sspec, rspec, rspec, rspec, rspec, rspec, cspec],
        out_specs=(sspec, cspec),
        compiler_params=_params("parallel", vmem=VMEM_LIMIT_BYTES),
        name="rwkv_step",
    )(state, heads(r), heads(k), heads(kk), heads(kka), heads(w), v_t)
    return y_t.transpose(0, 2, 1).reshape(n, rw_w), s_new


def _rwkv_step_post_kernel(y_ref, bonus_ref, g_ref, vec_ref, o_ref):
    o_ref[...] = _rw_post(y_ref[...], bonus_ref[...], g_ref[...], vec_ref[...], _head_ones()).astype(o_ref.dtype)


def _rwkv_step_post(y, bonus, g, vec):
    n, rw_w = y.shape
    spec = pl.BlockSpec((n, LANES), lambda h: (0, h))
    return pl.pallas_call(
        _rwkv_step_post_kernel,
        out_shape=jax.ShapeDtypeStruct((n, rw_w), BF16),
        grid=(rw_w // LANES,),
        in_specs=[spec, spec, spec, pl.BlockSpec((8, LANES), lambda h: (0, h))],
        out_specs=spec,
        compiler_params=_params("parallel"),
        name="rwkv_step_post",
    )(y, bonus, g, vec)


def _merge_kernel(oh_ref, or_ref, wh_ref, wr_ref, gh_ref, gr_ref, o_ref, whb_ref, wrb_ref):
    @pl.when(pl.program_id(1) == 0)
    def _():
        whb_ref[...] = wh_ref[...].astype(BF16)
        wrb_ref[...] = wr_ref[...].astype(BF16)

    m = (jax.nn.sigmoid(gh_ref[...]) * _nn(oh_ref[...], whb_ref[...])
         + jax.nn.sigmoid(gr_ref[...]) * _nn(or_ref[...], wrb_ref[...]))
    o_ref[...] = m.astype(o_ref.dtype)


def _merge(o_hg, o_rw, w_hg, w_rw, p_all, *, gate_col0, tm, tn):
    m, kh = o_hg.shape
    kr = o_rw.shape[1]
    n = w_hg.shape[1]
    gb = gate_col0 // tn
    return pl.pallas_call(
        _merge_kernel,
        out_shape=jax.ShapeDtypeStruct((m, n), BF16),
        grid=(n // tn, m // tm),
        in_specs=[pl.BlockSpec((tm, kh), lambda j, i: (i, 0)), pl.BlockSpec((tm, kr), lambda j, i: (i, 0)),
                  pl.BlockSpec((kh, tn), lambda j, i: (0, j)), pl.BlockSpec((kr, tn), lambda j, i: (0, j)),
                  pl.BlockSpec((tm, tn), lambda j, i: (i, gb + j)),
                  pl.BlockSpec((tm, tn), lambda j, i: (i, gb + n // tn + j))],
        out_specs=pl.BlockSpec((tm, tn), lambda j, i: (i, j)),
        scratch_shapes=[pltpu.VMEM((kh, tn), BF16), pltpu.VMEM((kr, tn), BF16)],
        compiler_params=_params("parallel", "arbitrary"),
        name="merge",
    )(o_hg, o_rw, w_hg, w_rw, p_all, p_all)


def _softmax_rows(s):
    e = jnp.exp(s - jnp.max(s, axis=-1, keepdims=True))
    return e / jnp.sum(e, axis=-1, keepdims=True)


def _attn_heads(q, kb_ref, vb_ref, o_ref, store):
    dh = q.shape[-1] // CA_HEADS
    scale = dh ** -0.5
    for h in range(CA_HEADS):
        hs = slice(h * dh, (h + 1) * dh)
        p = _softmax_rows(_nt(q[:, hs], kb_ref[:, hs]) * scale)
        store(hs, _nn(p.astype(BF16), vb_ref[:, hs]))


def _attn_prompt_kernel(q_ref, k_ref, v_ref, o_ref, kb_ref, vb_ref):
    @pl.when(pl.program_id(1) == 0)
    def _():
        kb_ref[...] = k_ref[0].astype(BF16)
        vb_ref[...] = v_ref[0].astype(BF16)

    def store(hs, o):
        o_ref[:, hs] = o.astype(o_ref.dtype)

    _attn_heads(q_ref[...], kb_ref, vb_ref, o_ref, store)


def _attn_prompt(q_all, mem_k, mem_v, *, batch, seq, tq):
    _, mt, d = mem_k.shape
    nl = seq // tq
    kv = pl.BlockSpec((1, mt, d), lambda b, l: (b, 0, 0))
    qs = pl.BlockSpec((tq, d), lambda b, l: (b * nl + l, 0))
    return pl.pallas_call(
        _attn_prompt_kernel,
        out_shape=jax.ShapeDtypeStruct((batch * seq, d), BF16),
        grid=(batch, nl),
        in_specs=[qs, kv, kv],
        out_specs=qs,
        scratch_shapes=[pltpu.VMEM((mt, d), BF16), pltpu.VMEM((mt, d), BF16)],
        compiler_params=_params("parallel", "arbitrary"),
        name="attn_prompt",
    )(q_all, mem_k, mem_v)


def _attn_step_kernel(q_ref, k_ref, v_ref, o_ref, kb_ref, vb_ref):
    kb_ref[...] = k_ref[0].astype(BF16)
    vb_ref[...] = v_ref[0].astype(BF16)
    q = jnp.broadcast_to(q_ref[0], (8, q_ref.shape[-1]))

    def store(hs, o):
        o_ref[0, :, hs] = o[0:1].astype(o_ref.dtype)

    _attn_heads(q, kb_ref, vb_ref, o_ref, store)


def _attn_step(q, mem_k, mem_v):
    n, mt, d = mem_k.shape
    kv = pl.BlockSpec((1, mt, d), lambda b: (b, 0, 0))
    qs = pl.BlockSpec((1, 1, d), lambda b: (b, 0, 0))
    o = pl.pallas_call(
        _attn_step_kernel,
        out_shape=jax.ShapeDtypeStruct((n, 1, d), BF16),
        grid=(n,),
        in_specs=[qs, kv, kv],
        out_specs=qs,
        scratch_shapes=[pltpu.VMEM((mt, d), BF16), pltpu.VMEM((mt, d), BF16)],
        compiler_params=_params("parallel"),
        name="attn_step",
    )(q.reshape(n, 1, d), mem_k, mem_v)
    return o.reshape(n, d)


_RT_E1, _RT_E2, _RT_C1, _RT_C2 = range(4)


def _router_kernel(x_ref, g_ref, wr_ref, br_ref, h_ref, rt_ref):
    x = x_ref[...]
    h = x * lax.rsqrt(jnp.mean(x * x, axis=-1, keepdims=True) + RMS_EPS) * g_ref[...]
    h_ref[...] = h
    logits = _dot3(h, wr_ref[...]) + br_ref[...]
    lane = _iota2(logits.shape, 1)
    lane_f = lane.astype(F32)
    neg = -jnp.inf
    first = lambda hit: jnp.min(jnp.where(hit, lane_f, float(LANES)), axis=-1, keepdims=True)

    is_group = lane < MOE_GROUPS
    gl = jnp.where(is_group, logits, neg)
    g_max = jnp.max(gl, axis=-1, keepdims=True)
    g_top = first(gl == g_max)
    g_w = 1.0 / jnp.sum(jnp.where(is_group, jnp.exp(logits - g_max), 0.0), axis=-1, keepdims=True)

    e_idx = lane - MOE_GROUPS
    in_group = (e_idx >= 0) & (e_idx < MOE_EXPERTS) & (jnp.right_shift(e_idx, MOE_PER_GROUP.bit_length() - 1).astype(F32) == g_top)
    el = jnp.where(in_group, logits, neg)
    m1 = jnp.max(el, axis=-1, keepdims=True)
    i1 = first(el == m1)
    el2 = jnp.where(lane_f == i1, neg, el)
    m2 = jnp.max(el2, axis=-1, keepdims=True)
    i2 = first(el2 == m2)
    t = jnp.exp(m2 - m1)
    w1 = 1.0 / (1.0 + t)
    w2 = t / (1.0 + t)
    rec = jnp.where(lane == _RT_E1, i1 - MOE_GROUPS, 0.0)
    rec = jnp.where(lane == _RT_E2, i2 - MOE_GROUPS, rec)
    rec = jnp.where(lane == _RT_C1, g_w * w1, rec)
    rec = jnp.where(lane == _RT_C2, g_w * w2, rec)
    rt_ref[...] = rec


def _router(x, g, w_rg, b_rg, w_re, b_re, *, tm):
    t, d = x.shape
    pad = LANES - MOE_GROUPS - MOE_EXPERTS
    wr = jnp.concatenate([w_rg, w_re, jnp.zeros((d, pad), F32)], axis=1)
    br = jnp.concatenate([b_rg, b_re, jnp.zeros((pad,), F32)]).reshape(1, LANES)
    return pl.pallas_call(
        _router_kernel,
        out_shape=(jax.ShapeDtypeStruct((t, d), F32), jax.ShapeDtypeStruct((t, LANES), F32)),
        grid=(t // tm,),
        in_specs=[pl.BlockSpec((tm, d), lambda i: (i, 0)), pl.BlockSpec((1, d), lambda i: (0, 0)),
                  pl.BlockSpec((d, LANES), lambda i: (0, 0)), pl.BlockSpec((1, LANES), lambda i: (0, 0))],
        out_specs=(pl.BlockSpec((tm, d), lambda i: (i, 0)), pl.BlockSpec((tm, LANES), lambda i: (i, 0))),
        compiler_params=_params("parallel", vmem=VMEM_LIMIT_BYTES),
        name="moe_router",
    )(x, g.reshape(1, d), wr, br)


def _dispatch_plan(e1, e2, n_tiles):
    t = e1.shape[0]
    keys = jnp.concatenate([e1, e2])
    onehot = (keys[:, None] == jnp.arange(MOE_EXPERTS, dtype=jnp.int32)[None, :]).astype(jnp.int32)
    rank = jnp.take_along_axis(jnp.cumsum(onehot, axis=0), keys[:, None], axis=1)[:, 0] - 1
    counts = jnp.sum(onehot, axis=0)
    tiles = (counts + MOE_TILE - 1) // MOE_TILE
    tile_end = jnp.cumsum(tiles)
    tile_start = tile_end - tiles
    dest = tile_start[keys] * MOE_TILE + rank
    token = jnp.concatenate([jnp.arange(t, dtype=jnp.int32)] * 2)
    row_src = jnp.zeros((n_tiles * MOE_TILE,), jnp.int32).at[dest].set(token)
    n_used = tile_end[-1].astype(jnp.int32)
    tile_id = jnp.arange(n_tiles, dtype=jnp.int32)
    tile_expert = jnp.sum(tile_id[:, None] >= tile_end[None, :], axis=1).astype(jnp.int32)
    tile_expert = jnp.minimum(tile_expert, tile_expert[jnp.maximum(n_used - 1, 0)])
    return row_src, tile_expert, n_used.reshape(1), dest.reshape(2, t).astype(jnp.int32)


def _expert_kernel(te_ref, nu_ref, src_ref, h_hbm, w1_ref, w3_ref, w2_ref, o_ref,
                   xbuf, sem, w1b, w3b, w2b):
    i = pl.program_id(0)
    n_used = nu_ref[0]

    def gather(tile, slot):
        def body(r, carry):
            row = src_ref[tile * MOE_TILE + r]
            pltpu.make_async_copy(h_hbm.at[pl.ds(row, 1)], xbuf.at[slot, pl.ds(r, 1)], sem.at[slot]).start()
            return carry
        lax.fori_loop(0, MOE_TILE, body, 0)

    @pl.when(i == 0)
    def _():
        gather(0, 0)

    @pl.when(i + 1 < n_used)
    def _():
        gather(i + 1, (i + 1) % 2)

    @pl.when(i < n_used)
    def _():
        slot = i % 2
        pltpu.make_async_copy(h_hbm.at[pl.ds(0, MOE_TILE)], xbuf.at[slot], sem.at[slot]).wait()

        @pl.when((i == 0) | (te_ref[i] != te_ref[jnp.maximum(i - 1, 0)]))
        def _():
            w1b[...] = w1_ref[0].astype(BF16)
            w3b[...] = w3_ref[0].astype(BF16)
            w2b[...] = w2_ref[0].astype(BF16)

        x = xbuf[slot].astype(BF16)
        hid = _silu(_nn(x, w1b[...])) * _nn(x, w3b[...])
        o_ref[...] = _nn(hid.astype(BF16), w2b[...])

    @pl.when(i >= n_used)
    def _():
        o_ref[...] = jnp.zeros_like(o_ref)


def _experts(h, w1, w3, w2, row_src, tile_expert, n_used):
    t, d = h.shape
    f = w1.shape[-1]
    n_tiles = tile_expert.shape[0]
    grid_spec = pltpu.PrefetchScalarGridSpec(
        num_scalar_prefetch=3,
        grid=(n_tiles,),
        in_specs=[pl.BlockSpec(memory_space=pl.ANY),
                  pl.BlockSpec((1, d, f), lambda i, te, nu, src: (te[i], 0, 0)),
                  pl.BlockSpec((1, d, f), lambda i, te, nu, src: (te[i], 0, 0)),
                  pl.BlockSpec((1, f, d), lambda i, te, nu, src: (te[i], 0, 0))],
        out_specs=pl.BlockSpec((MOE_TILE, d), lambda i, te, nu, src: (i, 0)),
        scratch_shapes=[pltpu.VMEM((2, MOE_TILE, d), F32), pltpu.SemaphoreType.DMA((2,)),
                        pltpu.VMEM((d, f), BF16), pltpu.VMEM((d, f), BF16), pltpu.VMEM((f, d), BF16)],
    )
    return pl.pallas_call(
        _expert_kernel,
        out_shape=jax.ShapeDtypeStruct((n_tiles * MOE_TILE, d), F32),
        grid_spec=grid_spec,
        compiler_params=_params("arbitrary", vmem=VMEM_LIMIT_BYTES),
        name="moe_experts",
    )(tile_expert, n_used, row_src, h, w1, w3, w2)


def _combine_kernel(pos_ref, x_ref, rt_ref, g_ref, ys_hbm, o_ref, ybuf, sem, *, tok0, n_tok):
    i = pl.program_id(0)
    tm = x_ref.shape[0]

    def gather(tile, slot):
        def body(r, carry):
            for s in range(2):
                row = pos_ref[s * n_tok + tok0 + tile * tm + r]
                pltpu.make_async_copy(ys_hbm.at[pl.ds(row, 1)], ybuf.at[slot, s, pl.ds(r, 1)], sem.at[slot]).start()
            return carry
        lax.fori_loop(0, tm, body, 0)

    @pl.when(i == 0)
    def _():
        gather(0, 0)

    @pl.when(i + 1 < pl.num_programs(0))
    def _():
        gather(i + 1, (i + 1) % 2)

    slot = i % 2
    for s in range(2):
        pltpu.make_async_copy(ys_hbm.at[pl.ds(0, tm)], ybuf.at[slot, s], sem.at[slot]).wait()
    rt = rt_ref[...]
    y = x_ref[...] + rt[:, _RT_C1:_RT_C1 + 1] * ybuf[slot, 0] + rt[:, _RT_C2:_RT_C2 + 1] * ybuf[slot, 1]
    o_ref[...] = y * lax.rsqrt(jnp.mean(y * y, axis=-1, keepdims=True) + RMS_EPS) * g_ref[...]


def _combine(x_all, rt, g, ys, pos, *, tok0, count, tm):
    n_tok, d = x_all.shape
    b0 = tok0 // tm
    grid_spec = pltpu.PrefetchScalarGridSpec(
        num_scalar_prefetch=1,
        grid=(count // tm,),
        in_specs=[pl.BlockSpec((tm, d), lambda i, pos: (b0 + i, 0)),
                  pl.BlockSpec((tm, LANES), lambda i, pos: (b0 + i, 0)),
                  pl.BlockSpec((1, d), lambda i, pos: (0, 0)),
                  pl.BlockSpec(memory_space=pl.ANY)],
        out_specs=pl.BlockSpec((tm, d), lambda i, pos: (i, 0)),
        scratch_shapes=[pltpu.VMEM((2, 2, tm, d), F32), pltpu.SemaphoreType.DMA((2,))],
    )
    return pl.pallas_call(
        functools.partial(_combine_kernel, tok0=tok0, n_tok=n_tok),
        out_shape=jax.ShapeDtypeStruct((count, d), F32),
        grid_spec=grid_spec,
        compiler_params=_params("arbitrary"),
        name="moe_combine",
    )(pos.reshape(-1), x_all, rt, g.reshape(1, d), ys)


def kernel(x_prompt, x_sample, mem_prompt, cache_mem_k, cache_mem_v, state_hgrn, state_rwkv, state_rwkv_shift, g_mix, w_in, hg_lb_logits, hg_norm_g, rw_mu, rw_w0, rw_w2, rw_a0, rw_a2, rw_g2, rw_kk, rw_ka, rw_rk, rw_ln_g, rw_ln_b, w_br_hg, w_br_rw, w_out, g_ca, g_mem, w_ca_q, w_ca_k, w_ca_v, w_ca_o, g_moe, w_rg, b_rg, w_re, b_re, w_e1, w_e3, w_e2, g_final):
    bp, seq, d = x_prompt.shape
    ns = x_sample.shape[0]
    tp = bp * seq
    t_all = tp + ns
    assert w_in.shape[0] == 1, "the final RMSNorm is fused into the last layer's MoE combine"
    layer = 0
    hg_w = HG_HEADS * HG_DIM
    rw_w = RW_HEADS * RW_DIM
    shift_w = state_rwkv_shift.shape[-1]
    rw_col = 4 * hg_w
    gate_col = rw_col + shift_w
    mem_t = mem_prompt.shape[1]
    tm = 640
    assert t_all % tm == 0

    lb = jnp.cumsum(jax.nn.softmax(hg_lb_logits.astype(F32), axis=0), axis=0)[layer]
    x_all = jnp.concatenate([x_prompt.reshape(tp, d), x_sample.reshape(ns, d)])

    h = _rmsnorm(x_all, g_mix[layer], BF16, tm)
    p_all = _matmul(h, w_in[layer], tm=tm, tn=1280, out_dtype=F32)
    gn = hg_norm_g[layer]
    o_hg_p, hg_p = _hgrn_prompt(p_all, lb, gn, batch=bp, seq=seq, col0=0, lblk=512)
    o_hg_s, hg_s = _hgrn_step(p_all, state_hgrn.reshape(state_hgrn.shape[1:]), lb, gn, row0=tp, col0=0)
    P = {'rw_w0': rw_w0[layer], 'rw_a0': rw_a0[layer], 'rw_kk': rw_kk[layer], 'rw_ka': rw_ka[layer],
         'rw_rk': rw_rk[layer], 'rw_ln_g': rw_ln_g[layer], 'rw_ln_b': rw_ln_b[layer],
         'rw_w2': rw_w2[layer], 'rw_a2': rw_a2[layer], 'rw_g2': rw_g2[layer]}
    vec, lora = _rw_param_blocks(P)
    mu = rw_mu[layer]
    o_rw_p, rw_p = _rwkv_prompt(p_all, jnp.zeros((bp, shift_w), F32), mu, vec, lora,
                                batch=bp, seq=seq, col0=rw_col // LANES, lblk=512)
    r_s, k_s, v_s, kk_s, kka_s, w_s, g_s, bonus_s = _rwkv_step_prep(
        p_all, state_rwkv_shift.reshape(ns, shift_w), mu, vec, lora, row0=tp, col0=rw_col // LANES)
    y_s, rw_s = _rwkv_step(state_rwkv.reshape(state_rwkv.shape[1:]), r_s, k_s, kk_s, kka_s, w_s, v_s)
    o_rw_s = _rwkv_step_post(y_s, bonus_s, g_s, vec)
    shift_p = p_all[seq - 1:tp:seq, rw_col:gate_col]
    shift_s = p_all[tp:, rw_col:gate_col]
    merged = _merge(jnp.concatenate([o_hg_p, o_hg_s]), jnp.concatenate([o_rw_p, o_rw_s]),
                    w_br_hg[layer], w_br_rw[layer], p_all, gate_col0=gate_col, tm=tm, tn=256)
    x1 = _matmul(merged, w_out[layer], tm=tm, tn=512, out_dtype=F32, res=x_all)

    q = _matmul(_rmsnorm(x1, g_ca[layer], BF16, tm), w_ca_q[layer], tm=tm, tn=512, out_dtype=BF16)
    m = _rmsnorm(mem_prompt.reshape(bp * mem_t, d), g_mem[layer], BF16, mem_t)
    mem_k = _matmul(m, w_ca_k[layer], tm=bp * mem_t, tn=512, out_dtype=F32)
    mem_v = _matmul(m, w_ca_v[layer], tm=bp * mem_t, tn=512, out_dtype=F32)
    o_p = _attn_prompt(q, mem_k.reshape(bp, mem_t, d), mem_v.reshape(bp, mem_t, d), batch=bp, seq=seq, tq=512)
    o_s = _attn_step(q[tp:], cache_mem_k.reshape(ns, mem_t, d), cache_mem_v.reshape(ns, mem_t, d))
    x2 = _matmul(jnp.concatenate([o_p, o_s]), w_ca_o[layer], tm=tm, tn=512, out_dtype=F32, res=x1)

    h3, rt = _router(x2, g_moe[layer], w_rg[layer], b_rg[layer], w_re[layer], b_re[layer], tm=tm)
    n_tiles = (2 * t_all + MOE_EXPERTS * (MOE_TILE - 1)) // MOE_TILE
    row_src, tile_expert, n_used, pos = _dispatch_plan(
        rt[:, _RT_E1].astype(jnp.int32), rt[:, _RT_E2].astype(jnp.int32), n_tiles)
    ys = _experts(h3, w_e1[layer], w_e3[layer], w_e2[layer], row_src, tile_expert, n_used)
    y_p = _combine(x2, rt, g_final, ys, pos, tok0=0, count=tp, tm=LANES)
    y_s = _combine(x2, rt, g_final, ys, pos, tok0=tp, count=ns, tm=LANES)

    ca_heads_shape = (1, bp, mem_t, CA_HEADS, d // CA_HEADS)
    return (y_p.reshape(bp, seq, d), y_s.reshape(ns, 1, d), hg_p[None], rw_p[None], shift_p[None],
            mem_k.reshape(ca_heads_shape), mem_v.reshape(ca_heads_shape), hg_s[None], rw_s[None], shift_s[None])
```

```python
import functools

import jax
import jax.numpy as jnp
from jax import lax
from jax.experimental import pallas as pl
from jax.experimental.pallas import tpu as pltpu

F32 = jnp.float32
BF16 = jnp.bfloat16

RMS_EPS = 1e-6
GN_EPS = 64e-5

LANES = 128
VMEM_LIMIT_BYTES = 56 * 1024 * 1024

HG_HEADS = 8
HG_DIM = 128
HG_CHUNK = 64
RW_HEADS = 16
RW_DIM = 64
RW_CHUNK = 64
RW_SUB = 16
CA_HEADS = 4
MOE_GROUPS = 4
MOE_PER_GROUP = 8
MOE_EXPERTS = MOE_GROUPS * MOE_PER_GROUP
MOE_TILE = 256


def _params(*sem, vmem=None):
    return pltpu.CompilerParams(dimension_semantics=sem, vmem_limit_bytes=vmem)


def _nt(a, b):
    return lax.dot_general(a, b, (((1,), (1,)), ((), ())), preferred_element_type=F32)


def _tn(a, b):
    return lax.dot_general(a, b, (((0,), (0,)), ((), ())), preferred_element_type=F32)


def _nn(a, b):
    return jnp.dot(a, b, preferred_element_type=F32)


def _split2(x):
    hi = x.astype(BF16)
    lo = (x - hi.astype(F32)).astype(BF16)
    return hi, lo


def _split3(x):
    hi = x.astype(BF16)
    r = x - hi.astype(F32)
    mid = r.astype(BF16)
    lo = (r - mid.astype(F32)).astype(BF16)
    return hi, mid, lo


def _dot3(a, b, dot=_nn):
    ah, al = _split2(a)
    bh, bl = _split2(b)
    return dot(ah, bh) + (dot(ah, bl) + dot(al, bh))


def _dot1(a, b):
    return _nn(a.astype(BF16), b.astype(BF16))


def _dot_exact_rhs(a, b_bf16, dot=_nn):
    h, m, l = _split3(a)
    return dot(h, b_bf16) + (dot(m, b_bf16) + dot(l, b_bf16))


def _iota2(shape, dim):
    return lax.broadcasted_iota(jnp.int32, shape, dim)


def _silu(x):
    return x * jax.nn.sigmoid(x)


def _rms_kernel(x_ref, g_ref, o_ref):
    x = x_ref[...]
    ms = jnp.mean(x * x, axis=-1, keepdims=True)
    o_ref[...] = (x * lax.rsqrt(ms + RMS_EPS) * g_ref[...]).astype(o_ref.dtype)


def _rmsnorm(x, g, out_dtype, tm):
    t, d = x.shape
    return pl.pallas_call(
        _rms_kernel,
        out_shape=jax.ShapeDtypeStruct((t, d), out_dtype),
        grid=(t // tm,),
        in_specs=[pl.BlockSpec((tm, d), lambda i: (i, 0)), pl.BlockSpec((1, d), lambda i: (0, 0))],
        out_specs=pl.BlockSpec((tm, d), lambda i: (i, 0)),
        compiler_params=_params("parallel"),
        name="rmsnorm",
    )(x, g.reshape(1, d))


def _mm_kernel(x_ref, w_ref, *rest, has_res):
    if has_res:
        r_ref, o_ref, wb_ref = rest
    else:
        o_ref, wb_ref = rest

    @pl.when(pl.program_id(1) == 0)
    def _():
        wb_ref[...] = w_ref[...].astype(BF16)

    acc = _nn(x_ref[...], wb_ref[...])
    if has_res:
        acc = r_ref[...] + acc
    o_ref[...] = acc.astype(o_ref.dtype)


def _matmul(x, w, *, tm, tn, out_dtype, res=None):
    m, k = x.shape
    n = w.shape[1]
    in_specs = [pl.BlockSpec((tm, k), lambda j, i: (i, 0)), pl.BlockSpec((k, tn), lambda j, i: (0, j))]
    args = [x, w]
    if res is not None:
        in_specs.append(pl.BlockSpec((tm, tn), lambda j, i: (i, j)))
        args.append(res)
    return pl.pallas_call(
        functools.partial(_mm_kernel, has_res=res is not None),
        out_shape=jax.ShapeDtypeStruct((m, n), out_dtype),
        grid=(n // tn, m // tm),
        in_specs=in_specs,
        out_specs=pl.BlockSpec((tm, tn), lambda j, i: (i, j)),
        scratch_shapes=[pltpu.VMEM((k, tn), BF16)],
        compiler_params=_params("parallel", "arbitrary", vmem=VMEM_LIMIT_BYTES),
        name="matmul",
    )(*args)


def _hgrn_gates(hq, hf, lb):
    q = _silu(hq)
    f = lb + (1.0 - lb) * jax.nn.sigmoid(hf)
    return q, f


def _hgrn_out(o, gn, hog):
    ms = jnp.mean(o * o, axis=-1, keepdims=True)
    return o * lax.rsqrt(ms + RMS_EPS) * gn * _silu(hog)


def _hgrn_prompt_kernel(q_ref, f_ref, i_ref, og_ref, lb_ref, gn_ref, o_ref, s_ref, st_ref, *, n_chunks):
    l = pl.program_id(2)
    c = HG_CHUNK

    @pl.when(l == 0)
    def _():
        st_ref[...] = jnp.zeros_like(st_ref)

    lb = lb_ref[...]
    gn = gn_ref[...]
    incl = _iota2((c, c), 0) >= _iota2((c, c), 1)
    tri = incl.astype(BF16)
    for ci in range(n_chunks):
        sl = slice(ci * c, (ci + 1) * c)
        q, f = _hgrn_gates(q_ref[sl, :], f_ref[sl, :], lb)
        k = 1.0 - f
        vb = i_ref[sl, :].astype(BF16)
        b = _cumsum_rows(jnp.log(f), tri)
        b_last = b[c - 1:c, :]
        q_in = (q * jnp.exp(b)).astype(BF16)
        k_in = (k * jnp.exp(-b)).astype(BF16)
        k_out = (k * jnp.exp(b_last - b)).astype(BF16)
        st = st_ref[...]
        o_inter = _nt(q_in, st.astype(BF16))
        att = jnp.where(incl, _nt(q_in, k_in), 0.0)
        o_intra = _nn(att.astype(BF16), vb)
        st_ref[...] = st * jnp.exp(b_last) + _tn(vb, k_out)
        o_ref[sl, :] = _hgrn_out(o_inter + o_intra, gn, og_ref[sl, :]).astype(o_ref.dtype)

    @pl.when(l == pl.num_programs(2) - 1)
    def _():
        s_ref[0, 0] = st_ref[...].T


def _cumsum_rows(x, tri_bf16):
    h, m, l = _split3(x)
    return _nn(tri_bf16, h) + (_nn(tri_bf16, m) + _nn(tri_bf16, l))


def _hgrn_prompt(p_all, lb, gn, *, batch, seq, col0, lblk):
    nl = seq // lblk
    hw = HG_HEADS * HG_DIM
    nb = hw // LANES

    def pspec(seg):
        return pl.BlockSpec((lblk, LANES), lambda b, h, l: (b * nl + l, col0 + seg * nb + h))

    vec = pl.BlockSpec((1, LANES), lambda b, h, l: (0, h))
    return pl.pallas_call(
        functools.partial(_hgrn_prompt_kernel, n_chunks=lblk // HG_CHUNK),
        out_shape=(jax.ShapeDtypeStruct((batch * seq, hw), BF16),
                   jax.ShapeDtypeStruct((batch, HG_HEADS, HG_DIM, HG_DIM), F32)),
        grid=(batch, HG_HEADS, nl),
        in_specs=[pspec(0), pspec(1), pspec(2), pspec(3), vec, vec],
        out_specs=(pl.BlockSpec((lblk, LANES), lambda b, h, l: (b * nl + l, h)),
                   pl.BlockSpec((1, 1, HG_DIM, HG_DIM), lambda b, h, l: (b, h, 0, 0))),
        scratch_shapes=[pltpu.VMEM((HG_DIM, HG_DIM), F32)],
        compiler_params=_params("parallel", "parallel", "arbitrary"),
        name="hgrn_prompt",
    )(p_all, p_all, p_all, p_all, lb.reshape(1, hw), gn.reshape(1, hw))


_RW_W0, _RW_A0, _RW_KK, _RW_KA, _RW_RK, _RW_LNG, _RW_LNB = range(7)


def _head_ones():
    same = (_iota2((LANES, LANES), 0) // RW_DIM) == (_iota2((LANES, LANES), 1) // RW_DIM)
    return same.astype(BF16)


def _head_sum(x, ones_bd):
    return _dot_exact_rhs(x, ones_bd)


def _rw_prep(xr, xk, xv, xc, vec, lora, ones_bd):
    c1 = xc[:, :LANES]
    w_lora = _nn(jnp.tanh(c1).astype(BF16), lora[0].astype(BF16))
    a_lora = _nn(c1.astype(BF16), lora[1].astype(BF16))
    g = _nn(jax.nn.sigmoid(xc[:, LANES:]).astype(BF16), lora[2].astype(BF16))
    z = -(vec[_RW_W0:_RW_W0 + 1] + w_lora)
    softplus = jnp.maximum(z, 0.0) + jnp.log1p(jnp.exp(-jnp.abs(z)))
    lw = -jnp.exp(-softplus - 0.5)
    a = jax.nn.sigmoid(vec[_RW_A0:_RW_A0 + 1] + a_lora)
    kk = xk * vec[_RW_KK:_RW_KK + 1]
    kk = kk / jnp.maximum(jnp.sqrt(_head_sum(kk * kk, ones_bd)), 1e-12)
    k = xk * (1.0 + (a - 1.0) * vec[_RW_KA:_RW_KA + 1])
    bonus = _head_sum(xr * k * vec[_RW_RK:_RW_RK + 1], ones_bd) * xv
    return xr, k, xv, kk, kk * a, lw, g, bonus


def _rw_post(y, bonus, g, vec, ones_bd):
    inv = 1.0 / RW_DIM
    mu = _head_sum(y, ones_bd) * inv
    d = y - mu
    var = _head_sum(d * d, ones_bd) * inv
    yn = d * lax.rsqrt(var + GN_EPS) * vec[_RW_LNG:_RW_LNG + 1] + vec[_RW_LNB:_RW_LNB + 1]
    return (yn + bonus) * g


def _rw_chunk_operands(r, k, v, kk, kka, lw, tri):
    c = r.shape[0]
    ginc = _cumsum_rows(lw, tri)
    glast = ginc[c - 1:c, :]
    e_neg = jnp.exp(-ginc)
    e_rem = jnp.exp(glast - ginc)
    a_hat = kk * jnp.exp(ginc - lw)
    r_hat = r * jnp.exp(ginc)
    e_last = jnp.exp(glast)
    b_hat, k_hat, b_til, k_til = kka * e_neg, k * e_neg, kka * e_rem, k * e_rem
    units = []
    for h in range(LANES // RW_DIM):
        s = slice(h * RW_DIM, (h + 1) * RW_DIM)
        units.append(dict(
            ah=a_hat[:, s], rh=r_hat[:, s], ab=a_hat[:, s].astype(BF16), rb=r_hat[:, s].astype(BF16),
            bb=b_hat[:, s].astype(BF16), kb=k_hat[:, s].astype(BF16), bt=b_til[:, s].astype(BF16),
            kt=k_til[:, s].astype(BF16), vb=v[:, s].astype(BF16), e_last=e_last[:, s]))
    return units


def _rw_local_levels(us, out):
    c = us[0]['ab'].shape[0]
    row, col = _iota2((c, c), 0), _iota2((c, c), 1)
    incl, strict = row >= col, row > col
    diag_blk = (row // RW_SUB) == (col // RW_SUB)
    eye_c = (row == col).astype(F32)
    eye_d = _iota2((RW_DIM, RW_DIM), 0) == _iota2((RW_DIM, RW_DIM), 1)

    lm = [jnp.where(strict, _nt(u['ab'], u['bb']), 0.0) for u in us]
    lk = [jnp.where(strict, _nt(u['ab'], u['kb']), 0.0).astype(BF16) for u in us]
    pb = [jnp.where(incl, _nt(u['rb'], u['bb']), 0.0).astype(BF16) for u in us]
    pk = [jnp.where(incl, _nt(u['rb'], u['kb']), 0.0).astype(BF16) for u in us]
    yield
    pw = [jnp.where(diag_blk, -x, 0.0) for x in lm]
    e = [jnp.where(diag_blk, 0.0, x) for x in lm]
    t_d = [eye_c + x for x in pw]
    x = [jnp.concatenate([u['ah'], _nn(lk_, u['vb'])], axis=1) for u, lk_ in zip(us, lk)]
    for _ in range(RW_SUB.bit_length() - 2):
        pw = [_dot1(p, p) for p in pw]
        yield
        t_d = [t + _dot1(t, p) for t, p in zip(t_d, pw)]
    yield
    m = [_dot1(t, e_) for t, e_ in zip(t_d, e)]
    tdx = [_dot1(t, x_) for t, x_ in zip(t_d, x)]
    yield
    acc = [eye_c - m_ for m_ in m]
    pw = m
    for _ in range((c // RW_SUB).bit_length() - 2):
        pw = [_dot1(p, p) for p in pw]
        yield
        acc = [a + _dot1(a, p) for a, p in zip(acc, pw)]
        yield
    txb = [_dot1(a, t).astype(BF16) for a, t in zip(acc, tdx)]
    yield
    p_au = [_nn(p, t) for p, t in zip(pb, txb)]
    b_au = [_tn(u['bt'], t) for u, t in zip(us, txb)]
    pkv = [_nn(p, u['vb']) for u, p in zip(us, pk)]
    ktv = [_tn(u['kt'], u['vb']) for u in us]
    yield
    for i, u in enumerate(us):
        out.append((u['rh'] - p_au[i][:, :RW_DIM], pkv[i] - p_au[i][:, RW_DIM:],
                    jnp.where(eye_d, u['e_last'], 0.0) - b_au[i][:, :RW_DIM], ktv[i] - b_au[i][:, RW_DIM:]))


RW_GROUP_CHUNKS = 4


def _rwkv_prompt_kernel(pr_ref, pk_ref, pv_ref, pc_ref, sr_ref, sk_ref, sv_ref, sc_ref,
                        mr_ref, mk_ref, mv_ref, mc_ref, vec_ref, lora_ref,
                        o_ref, s_ref, a_ref, y_ref, cr_ref, ck_ref, cv_ref, cc_ref, *, n_chunks):
    l = pl.program_id(2)
    lblk = pr_ref.shape[0]
    c = RW_CHUNK

    @pl.when(l == 0)
    def _():
        a_ref[...] = jnp.zeros_like(a_ref)
        cr_ref[...] = sr_ref[0]
        ck_ref[...] = sk_ref[0]
        cv_ref[...] = sv_ref[0]
        cc_ref[...] = sc_ref[0]

    def shift_mix(x_ref, carry_ref, mu_ref):
        x = x_ref[...]
        first = _iota2(x.shape, 0) == 0
        prev = jnp.where(first, carry_ref[...], pltpu.roll(x, 1, 0))
        carry_ref[...] = x[lblk - 1:lblk, :]
        return x + mu_ref[...] * (prev - x)

    ones_bd = _head_ones()
    vec = vec_ref[...]
    r, k, v, kk, kka, lw, g, bonus = _rw_prep(
        shift_mix(pr_ref, cr_ref, mr_ref), shift_mix(pk_ref, ck_ref, mk_ref), shift_mix(pv_ref, cv_ref, mv_ref),
        shift_mix(pc_ref, cc_ref, mc_ref), vec, lora_ref[...], ones_bd)
    tri = (_iota2((c, c), 0) >= _iota2((c, c), 1)).astype(BF16)
    n_heads = LANES // RW_DIM
    state = [a_ref[h] for h in range(n_heads)]

    def state_step(ci, local):
        ys = []
        for h in range(n_heads):
            r_til, y_loc, g_mat, a_loc = local[h]
            ys.append(_dot3(r_til, state[h]) + y_loc)
            state[h] = _dot3(g_mat, state[h]) + a_loc
        y_ref[ci * c:(ci + 1) * c, :] = jnp.concatenate(ys, axis=1)

    pending = []
    for g0 in range(0, n_chunks, RW_GROUP_CHUNKS):
        chunks = range(g0, min(g0 + RW_GROUP_CHUNKS, n_chunks))
        units = []
        for ci in chunks:
            sl = slice(ci * c, (ci + 1) * c)
            units += _rw_chunk_operands(r[sl], k[sl], v[sl], kk[sl], kka[sl], lw[sl], tri)
        local = []
        for _ in _rw_local_levels(units, local):
            if pending:
                pending.pop(0)()
        pending += [functools.partial(state_step, ci, local[i * n_heads:(i + 1) * n_heads])
                    for i, ci in enumerate(chunks)]
    for step in pending:
        step()
    for h in range(n_heads):
        a_ref[h] = state[h]
    o_ref[...] = _rw_post(y_ref[...], bonus, g, vec, ones_bd).astype(o_ref.dtype)

    @pl.when(l == pl.num_programs(2) - 1)
    def _():
        for h in range(n_heads):
            s_ref[0, h] = state[h].T


def _rw_param_blocks(P):
    rw_w = RW_HEADS * RW_DIM
    vec = jnp.stack([P['rw_w0'], P['rw_a0'], P['rw_kk'], P['rw_ka'], P['rw_rk'], P['rw_ln_g'], P['rw_ln_b'],
                     jnp.zeros((rw_w,), F32)])
    z = jnp.zeros_like(P['rw_w2'])
    lora = jnp.stack([jnp.concatenate([P['rw_w2'], z]), jnp.concatenate([z, P['rw_a2']]), P['rw_g2']])
    return vec, lora


def _rwkv_prompt(p_all, shift0, mu, vec, lora, *, batch, seq, col0, lblk):
    nl = seq // lblk
    rw_w = RW_HEADS * RW_DIM
    nb = rw_w // LANES
    code_blk = (col0 + 3 * nb) // 2
    shift_w = shift0.shape[-1]

    def pspec(seg):
        return pl.BlockSpec((lblk, LANES), lambda b, h, l: (b * nl + l, col0 + seg * nb + h))

    def sspec(seg):
        return pl.BlockSpec((1, 1, LANES), lambda b, h, l: (b, 0, seg * nb + h))

    def mspec(seg):
        return pl.BlockSpec((1, LANES), lambda b, h, l: (0, seg * nb + h))

    in_specs = [pspec(0), pspec(1), pspec(2),
                pl.BlockSpec((lblk, 2 * LANES), lambda b, h, l: (b * nl + l, code_blk)),
                sspec(0), sspec(1), sspec(2),
                pl.BlockSpec((1, 1, 2 * LANES), lambda b, h, l: (b, 0, 3 * nb // 2)),
                mspec(0), mspec(1), mspec(2),
                pl.BlockSpec((1, 2 * LANES), lambda b, h, l: (0, 3 * nb // 2)),
                pl.BlockSpec((8, LANES), lambda b, h, l: (0, h)),
                pl.BlockSpec((3, LANES, LANES), lambda b, h, l: (0, 0, h))]
    sh3 = shift0.reshape(batch, 1, shift_w)
    mu2 = mu.reshape(1, shift_w)
    return pl.pallas_call(
        functools.partial(_rwkv_prompt_kernel, n_chunks=lblk // RW_CHUNK),
        out_shape=(jax.ShapeDtypeStruct((batch * seq, rw_w), BF16),
                   jax.ShapeDtypeStruct((batch, RW_HEADS, RW_DIM, RW_DIM), F32)),
        grid=(batch, nb, nl),
        in_specs=in_specs,
        out_specs=(pl.BlockSpec((lblk, LANES), lambda b, h, l: (b * nl + l, h)),
                   pl.BlockSpec((1, 2, RW_DIM, RW_DIM), lambda b, h, l: (b, h, 0, 0))),
        scratch_shapes=[pltpu.VMEM((2, RW_DIM, RW_DIM), F32), pltpu.VMEM((lblk, LANES), F32),
                        pltpu.VMEM((1, LANES), F32), pltpu.VMEM((1, LANES), F32), pltpu.VMEM((1, LANES), F32),
                        pltpu.VMEM((1, 2 * LANES), F32)],
        compiler_params=_params("parallel", "parallel", "arbitrary"),
        name="rwkv_prompt",
    )(p_all, p_all, p_all, p_all, sh3, sh3, sh3, sh3, mu2, mu2, mu2, mu2, vec, lora)


STEP_REQ = 16


def _transpose_rows(x):
    r = x.shape[0]
    eye = (_iota2((r, r), 0) == _iota2((r, r), 1)).astype(BF16)
    return _dot_exact_rhs(x, eye, dot=_tn)


def _hgrn_step_kernel(q_ref, f_ref, i_ref, og_ref, lb_ref, gn_ref, s_ref, o_ref, so_ref):
    q, f = _hgrn_gates(q_ref[...], f_ref[...], lb_ref[...])
    f_t = _transpose_rows(f)
    q_t = _transpose_rows(q)
    v = i_ref[...]
    rows = []
    for j in range(STEP_REQ):
        fc = f_t[:, j:j + 1]
        s_new = s_ref[j, 0] * fc + (1.0 - fc) * v[j:j + 1, :]
        so_ref[j, 0] = s_new
        rows.append(jnp.sum(s_new * q_t[:, j:j + 1], axis=0, keepdims=True))
    o = jnp.concatenate(rows, axis=0)
    o_ref[...] = _hgrn_out(o, gn_ref[...], og_ref[...]).astype(o_ref.dtype)


def _hgrn_step(p_all, state, lb, gn, *, row0, col0):
    n = state.shape[0]
    hw = HG_HEADS * HG_DIM
    nb = hw // LANES
    rb0 = row0 // STEP_REQ

    def pspec(seg):
        return pl.BlockSpec((STEP_REQ, LANES), lambda i, h: (rb0 + i, col0 + seg * nb + h))

    vec = pl.BlockSpec((1, LANES), lambda i, h: (0, h))
    sspec = pl.BlockSpec((STEP_REQ, 1, HG_DIM, HG_DIM), lambda i, h: (i, h, 0, 0))
    return pl.pallas_call(
        _hgrn_step_kernel,
        out_shape=(jax.ShapeDtypeStruct((n, hw), BF16), jax.ShapeDtypeStruct(state.shape, F32)),
        grid=(n // STEP_REQ, HG_HEADS),
        in_specs=[pspec(0), pspec(1), pspec(2), pspec(3), vec, vec, sspec],
        out_specs=(pl.BlockSpec((STEP_REQ, LANES), lambda i, h: (i, h)), sspec),
        compiler_params=_params("parallel", "parallel"),
        name="hgrn_step",
    )(p_all, p_all, p_all, p_all, lb.reshape(1, hw), gn.reshape(1, hw), state)


def _rwkv_step_prep_kernel(pr_ref, pk_ref, pv_ref, pc_ref, sr_ref, sk_ref, sv_ref, sc_ref,
                           mr_ref, mk_ref, mv_ref, mc_ref, vec_ref, lora_ref,
                           r_ref, k_ref, v_ref, kk_ref, kka_ref, w_ref, g_ref, bonus_ref):
    def mix(x_ref, prev_ref, mu_ref):
        x = x_ref[...]
        return x + mu_ref[...] * (prev_ref[...] - x)

    r, k, v, kk, kka, lw, g, bonus = _rw_prep(
        mix(pr_ref, sr_ref, mr_ref), mix(pk_ref, sk_ref, mk_ref), mix(pv_ref, sv_ref, mv_ref),
        mix(pc_ref, sc_ref, mc_ref), vec_ref[...], lora_ref[...], _head_ones())
    r_ref[...] = r
    k_ref[...] = k
    v_ref[...] = v
    kk_ref[...] = kk
    kka_ref[...] = kka
    w_ref[...] = jnp.exp(lw)
    g_ref[...] = g
    bonus_ref[...] = bonus


def _rwkv_step_prep(p_all, shift, mu, vec, lora, *, row0, col0):
    n, shift_w = shift.shape
    rw_w = RW_HEADS * RW_DIM
    nb = rw_w // LANES
    rb0 = row0 // n
    code_blk = (col0 + 3 * nb) // 2

    def pspec(seg):
        return pl.BlockSpec((n, LANES), lambda h: (rb0, col0 + seg * nb + h))

    def sspec(seg):
        return pl.BlockSpec((n, LANES), lambda h: (0, seg * nb + h))

    def mspec(seg):
        return pl.BlockSpec((1, LANES), lambda h: (0, seg * nb + h))

    ospec = pl.BlockSpec((n, LANES), lambda h: (0, h))
    mu2 = mu.reshape(1, shift_w)
    return pl.pallas_call(
        _rwkv_step_prep_kernel,
        out_shape=tuple(jax.ShapeDtypeStruct((n, rw_w), F32) for _ in range(8)),
        grid=(nb,),
        in_specs=[pspec(0), pspec(1), pspec(2), pl.BlockSpec((n, 2 * LANES), lambda h: (rb0, code_blk)),
                  sspec(0), sspec(1), sspec(2), pl.BlockSpec((n, 2 * LANES), lambda h: (0, 3 * nb // 2)),
                  mspec(0), mspec(1), mspec(2), pl.BlockSpec((1, 2 * LANES), lambda h: (0, 3 * nb // 2)),
                  pl.BlockSpec((8, LANES), lambda h: (0, h)), pl.BlockSpec((3, LANES, LANES), lambda h: (0, 0, h))],
        out_specs=tuple(ospec for _ in range(8)),
        compiler_params=_params("parallel"),
        name="rwkv_step_prep",
    )(p_all, p_all, p_all, p_all, shift, shift, shift, shift, mu2, mu2, mu2, mu2, vec, lora)


def _rwkv_step_kernel(s_ref, r_ref, k_ref, kk_ref, kka_ref, w_ref, vt_ref, so_ref, yt_ref):
    def head(h, carry):
        rows = pl.ds(pl.multiple_of(h * RW_DIM, RW_DIM), RW_DIM)
        for j in range(STEP_REQ):
            s = s_ref[j, h]
            sa = jnp.sum(s * kk_ref[j, pl.ds(h, 1), :], axis=1, keepdims=True)
            s_new = (s * w_ref[j, pl.ds(h, 1), :] - sa * kka_ref[j, pl.ds(h, 1), :]
                     + vt_ref[0, rows, j:j + 1] * k_ref[j, pl.ds(h, 1), :])
            so_ref[j, h] = s_new
            yt_ref[0, rows, j:j + 1] = jnp.sum(s_new * r_ref[j, pl.ds(h, 1), :], axis=1, keepdims=True)
        return carry

    lax.fori_loop(0, RW_HEADS, head, 0)


def _rwkv_step(state, r, k, kk, kka, w, v):
    n = state.shape[0]
    rw_w = RW_HEADS * RW_DIM
    nblk = n // STEP_REQ
    heads = lambda t: t.reshape(n, RW_HEADS, RW_DIM)
    v_t = v.reshape(nblk, STEP_REQ, rw_w).transpose(0, 2, 1)
    sspec = pl.BlockSpec((STEP_REQ, RW_HEADS, RW_DIM, RW_DIM), lambda i: (i, 0, 0, 0))
    rspec = pl.BlockSpec((STEP_REQ, RW_HEADS, RW_DIM), lambda i: (i, 0, 0))
    cspec = pl.BlockSpec((1, rw_w, STEP_REQ), lambda i: (i, 0, 0))
    s_new, y_t = pl.pallas_call(
        _rwkv_step_kernel,
        out_shape=(jax.ShapeDtypeStruct(state.shape, F32), jax.ShapeDtypeStruct((nblk, rw_w, STEP_REQ), F32)),
        grid=(nblk,),
        in_specs=[sspec, rspec, rspec, rspec, rspec, rspec, cspec],
        out_specs=(sspec, cspec),
        compiler_params=_params("parallel", vmem=VMEM_LIMIT_BYTES),
        name="rwkv_step",
    )(state, heads(r), heads(k), heads(kk), heads(kka), heads(w), v_t)
    return y_t.transpose(0, 2, 1).reshape(n, rw_w), s_new


def _rwkv_step_post_kernel(y_ref, bonus_ref, g_ref, vec_ref, o_ref):
    o_ref[...] = _rw_post(y_ref[...], bonus_ref[...], g_ref[...], vec_ref[...], _head_ones()).astype(o_ref.dtype)


def _rwkv_step_post(y, bonus, g, vec):
    n, rw_w = y.shape
    spec = pl.BlockSpec((n, LANES), lambda h: (0, h))
    return pl.pallas_call(
        _rwkv_step_post_kernel,
        out_shape=jax.ShapeDtypeStruct((n, rw_w), BF16),
        grid=(rw_w // LANES,),
        in_specs=[spec, spec, spec, pl.BlockSpec((8, LANES), lambda h: (0, h))],
        out_specs=spec,
        compiler_params=_params("parallel"),
        name="rwkv_step_post",
    )(y, bonus, g, vec)


def _merge_kernel(oh_ref, or_ref, wh_ref, wr_ref, gh_ref, gr_ref, o_ref, whb_ref, wrb_ref):
    @pl.when(pl.program_id(1) == 0)
    def _():
        whb_ref[...] = wh_ref[...].astype(BF16)
        wrb_ref[...] = wr_ref[...].astype(BF16)

    m = (jax.nn.sigmoid(gh_ref[...]) * _nn(oh_ref[...], whb_ref[...])
         + jax.nn.sigmoid(gr_ref[...]) * _nn(or_ref[...], wrb_ref[...]))
    o_ref[...] = m.astype(o_ref.dtype)


def _merge(o_hg, o_rw, w_hg, w_rw, p_all, *, gate_col0, tm, tn):
    m, kh = o_hg.shape
    kr = o_rw.shape[1]
    n = w_hg.shape[1]
    gb = gate_col0 // tn
    return pl.pallas_call(
        _merge_kernel,
        out_shape=jax.ShapeDtypeStruct((m, n), BF16),
        grid=(n // tn, m // tm),
        in_specs=[pl.BlockSpec((tm, kh), lambda j, i: (i, 0)), pl.BlockSpec((tm, kr), lambda j, i: (i, 0)),
                  pl.BlockSpec((kh, tn), lambda j, i: (0, j)), pl.BlockSpec((kr, tn), lambda j, i: (0, j)),
                  pl.BlockSpec((tm, tn), lambda j, i: (i, gb + j)),
                  pl.BlockSpec((tm, tn), lambda j, i: (i, gb + n // tn + j))],
        out_specs=pl.BlockSpec((tm, tn), lambda j, i: (i, j)),
        scratch_shapes=[pltpu.VMEM((kh, tn), BF16), pltpu.VMEM((kr, tn), BF16)],
        compiler_params=_params("parallel", "arbitrary"),
        name="merge",
    )(o_hg, o_rw, w_hg, w_rw, p_all, p_all)


def _softmax_rows(s):
    e = jnp.exp(s - jnp.max(s, axis=-1, keepdims=True))
    return e / jnp.sum(e, axis=-1, keepdims=True)


def _attn_heads(q, kb_ref, vb_ref, o_ref, store):
    dh = q.shape[-1] // CA_HEADS
    scale = dh ** -0.5
    for h in range(CA_HEADS):
        hs = slice(h * dh, (h + 1) * dh)
        p = _softmax_rows(_nt(q[:, hs], kb_ref[:, hs]) * scale)
        store(hs, _nn(p.astype(BF16), vb_ref[:, hs]))


def _attn_prompt_kernel(q_ref, k_ref, v_ref, o_ref, kb_ref, vb_ref):
    @pl.when(pl.program_id(1) == 0)
    def _():
        kb_ref[...] = k_ref[0].astype(BF16)
        vb_ref[...] = v_ref[0].astype(BF16)

    def store(hs, o):
        o_ref[:, hs] = o.astype(o_ref.dtype)

    _attn_heads(q_ref[...], kb_ref, vb_ref, o_ref, store)


def _attn_prompt(q_all, mem_k, mem_v, *, batch, seq, tq):
    _, mt, d = mem_k.shape
    nl = seq // tq
    kv = pl.BlockSpec((1, mt, d), lambda b, l: (b, 0, 0))
    qs = pl.BlockSpec((tq, d), lambda b, l: (b * nl + l, 0))
    return pl.pallas_call(
        _attn_prompt_kernel,
        out_shape=jax.ShapeDtypeStruct((batch * seq, d), BF16),
        grid=(batch, nl),
        in_specs=[qs, kv, kv],
        out_specs=qs,
        scratch_shapes=[pltpu.VMEM((mt, d), BF16), pltpu.VMEM((mt, d), BF16)],
        compiler_params=_params("parallel", "arbitrary"),
        name="attn_prompt",
    )(q_all, mem_k, mem_v)


def _attn_step_kernel(q_ref, k_ref, v_ref, o_ref):
    q = jnp.broadcast_to(q_ref[0], (8, q_ref.shape[-1]))

    def store(hs, o):
        o_ref[0, :, hs] = o[0:1].astype(o_ref.dtype)

    _attn_heads(q, k_ref.at[0], v_ref.at[0], o_ref, store)


def _attn_step(q, mem_k, mem_v):
    n, mt, d = mem_k.shape
    kv = pl.BlockSpec((1, mt, d), lambda b: (b, 0, 0))
    qs = pl.BlockSpec((1, 1, d), lambda b: (b, 0, 0))
    o = pl.pallas_call(
        _attn_step_kernel,
        out_shape=jax.ShapeDtypeStruct((n, 1, d), BF16),
        grid=(n,),
        in_specs=[qs, kv, kv],
        out_specs=qs,
        compiler_params=_params("parallel"),
        name="attn_step",
    )(q.reshape(n, 1, d), mem_k, mem_v)
    return o.reshape(n, d)


_RT_E1, _RT_E2, _RT_C1, _RT_C2 = range(4)


def _router_kernel(x_ref, g_ref, wr_ref, br_ref, h_ref, rt_ref):
    x = x_ref[...]
    h = x * lax.rsqrt(jnp.mean(x * x, axis=-1, keepdims=True) + RMS_EPS) * g_ref[...]
    h_ref[...] = h
    logits = _dot3(h, wr_ref[...]) + br_ref[...]
    lane = _iota2(logits.shape, 1)
    lane_f = lane.astype(F32)
    neg = -jnp.inf
    first = lambda hit: jnp.min(jnp.where(hit, lane_f, float(LANES)), axis=-1, keepdims=True)

    is_group = lane < MOE_GROUPS
    gl = jnp.where(is_group, logits, neg)
    g_max = jnp.max(gl, axis=-1, keepdims=True)
    g_top = first(gl == g_max)
    g_w = 1.0 / jnp.sum(jnp.where(is_group, jnp.exp(logits - g_max), 0.0), axis=-1, keepdims=True)

    e_idx = lane - MOE_GROUPS
    in_group = (e_idx >= 0) & (e_idx < MOE_EXPERTS) & (jnp.right_shift(e_idx, MOE_PER_GROUP.bit_length() - 1).astype(F32) == g_top)
    el = jnp.where(in_group, logits, neg)
    m1 = jnp.max(el, axis=-1, keepdims=True)
    i1 = first(el == m1)
    el2 = jnp.where(lane_f == i1, neg, el)
    m2 = jnp.max(el2, axis=-1, keepdims=True)
    i2 = first(el2 == m2)
    t = jnp.exp(m2 - m1)
    w1 = 1.0 / (1.0 + t)
    w2 = t / (1.0 + t)
    rec = jnp.where(lane == _RT_E1, i1 - MOE_GROUPS, 0.0)
    rec = jnp.where(lane == _RT_E2, i2 - MOE_GROUPS, rec)
    rec = jnp.where(lane == _RT_C1, g_w * w1, rec)
    rec = jnp.where(lane == _RT_C2, g_w * w2, rec)
    rt_ref[...] = rec


def _router(x, g, w_rg, b_rg, w_re, b_re, *, tm):
    t, d = x.shape
    pad = LANES - MOE_GROUPS - MOE_EXPERTS
    wr = jnp.concatenate([w_rg, w_re, jnp.zeros((d, pad), F32)], axis=1)
    br = jnp.concatenate([b_rg, b_re, jnp.zeros((pad,), F32)]).reshape(1, LANES)
    return pl.pallas_call(
        _router_kernel,
        out_shape=(jax.ShapeDtypeStruct((t, d), F32), jax.ShapeDtypeStruct((t, LANES), F32)),
        grid=(t // tm,),
        in_specs=[pl.BlockSpec((tm, d), lambda i: (i, 0)), pl.BlockSpec((1, d), lambda i: (0, 0)),
                  pl.BlockSpec((d, LANES), lambda i: (0, 0)), pl.BlockSpec((1, LANES), lambda i: (0, 0))],
        out_specs=(pl.BlockSpec((tm, d), lambda i: (i, 0)), pl.BlockSpec((tm, LANES), lambda i: (i, 0))),
        compiler_params=_params("parallel", vmem=VMEM_LIMIT_BYTES),
        name="moe_router",
    )(x, g.reshape(1, d), wr, br)


def _dispatch_plan(e1, e2, n_tiles):
    t = e1.shape[0]
    keys = jnp.concatenate([e1, e2])
    onehot = (keys[:, None] == jnp.arange(MOE_EXPERTS, dtype=jnp.int32)[None, :]).astype(jnp.int32)
    rank = jnp.take_along_axis(jnp.cumsum(onehot, axis=0), keys[:, None], axis=1)[:, 0] - 1
    counts = jnp.sum(onehot, axis=0)
    tiles = (counts + MOE_TILE - 1) // MOE_TILE
    tile_end = jnp.cumsum(tiles)
    tile_start = tile_end - tiles
    dest = tile_start[keys] * MOE_TILE + rank
    token = jnp.concatenate([jnp.arange(t, dtype=jnp.int32)] * 2)
    row_src = jnp.zeros((n_tiles * MOE_TILE,), jnp.int32).at[dest].set(token)
    n_used = tile_end[-1].astype(jnp.int32)
    tile_id = jnp.arange(n_tiles, dtype=jnp.int32)
    tile_expert = jnp.sum(tile_id[:, None] >= tile_end[None, :], axis=1).astype(jnp.int32)
    tile_expert = jnp.minimum(tile_expert, tile_expert[jnp.maximum(n_used - 1, 0)])
    return row_src, tile_expert, n_used.reshape(1), dest.reshape(2, t).astype(jnp.int32)


def _expert_kernel(te_ref, nu_ref, src_ref, h_hbm, w1_ref, w3_ref, w2_ref, o_ref,
                   xbuf, sem, w1b, w3b, w2b):
    i = pl.program_id(0)
    n_used = nu_ref[0]

    def gather(tile, slot):
        def body(r, carry):
            row = src_ref[tile * MOE_TILE + r]
            pltpu.make_async_copy(h_hbm.at[pl.ds(row, 1)], xbuf.at[slot, pl.ds(r, 1)], sem.at[slot]).start()
            return carry
        lax.fori_loop(0, MOE_TILE, body, 0, unroll=8)

    @pl.when(i == 0)
    def _():
        gather(0, 0)

    @pl.when(i + 1 < n_used)
    def _():
        gather(i + 1, (i + 1) % 2)

    @pl.when(i < n_used)
    def _():
        slot = i % 2
        pltpu.make_async_copy(h_hbm.at[pl.ds(0, MOE_TILE)], xbuf.at[slot], sem.at[slot]).wait()

        @pl.when((i == 0) | (te_ref[i] != te_ref[jnp.maximum(i - 1, 0)]))
        def _():
            w1b[...] = w1_ref[0].astype(BF16)
            w3b[...] = w3_ref[0].astype(BF16)
            w2b[...] = w2_ref[0].astype(BF16)

        x = xbuf[slot].astype(BF16)
        hid = _silu(_nn(x, w1b[...])) * _nn(x, w3b[...])
        o_ref[...] = _nn(hid.astype(BF16), w2b[...])

    @pl.when(i >= n_used)
    def _():
        o_ref[...] = jnp.zeros_like(o_ref)


def _experts(h, w1, w3, w2, row_src, tile_expert, n_used):
    t, d = h.shape
    f = w1.shape[-1]
    n_tiles = tile_expert.shape[0]
    grid_spec = pltpu.PrefetchScalarGridSpec(
        num_scalar_prefetch=3,
        grid=(n_tiles,),
        in_specs=[pl.BlockSpec(memory_space=pl.ANY),
                  pl.BlockSpec((1, d, f), lambda i, te, nu, src: (te[i], 0, 0)),
                  pl.BlockSpec((1, d, f), lambda i, te, nu, src: (te[i], 0, 0)),
                  pl.BlockSpec((1, f, d), lambda i, te, nu, src: (te[i], 0, 0))],
        out_specs=pl.BlockSpec((MOE_TILE, d), lambda i, te, nu, src: (i, 0)),
        scratch_shapes=[pltpu.VMEM((2, MOE_TILE, d), F32), pltpu.SemaphoreType.DMA((2,)),
                        pltpu.VMEM((d, f), BF16), pltpu.VMEM((d, f), BF16), pltpu.VMEM((f, d), BF16)],
    )
    return pl.pallas_call(
        _expert_kernel,
        out_shape=jax.ShapeDtypeStruct((n_tiles * MOE_TILE, d), F32),
        grid_spec=grid_spec,
        compiler_params=_params("arbitrary", vmem=VMEM_LIMIT_BYTES),
        name="moe_experts",
    )(tile_expert, n_used, row_src, h, w1, w3, w2)


def _combine_kernel(pos_ref, x_ref, rt_ref, g_ref, ys_hbm, o_ref, ybuf, sem, *, tok0, n_tok):
    i = pl.program_id(0)
    tm = x_ref.shape[0]

    def gather(tile, slot):
        def body(r, carry):
            for s in range(2):
                row = pos_ref[s * n_tok + tok0 + tile * tm + r]
                pltpu.make_async_copy(ys_hbm.at[pl.ds(row, 1)], ybuf.at[slot, s, pl.ds(r, 1)], sem.at[slot]).start()
            return carry
        lax.fori_loop(0, tm, body, 0, unroll=4)

    @pl.when(i == 0)
    def _():
        gather(0, 0)

    @pl.when(i + 1 < pl.num_programs(0))
    def _():
        gather(i + 1, (i + 1) % 2)

    slot = i % 2
    for s in range(2):
        pltpu.make_async_copy(ys_hbm.at[pl.ds(0, tm)], ybuf.at[slot, s], sem.at[slot]).wait()
    rt = rt_ref[...]
    y = x_ref[...] + rt[:, _RT_C1:_RT_C1 + 1] * ybuf[slot, 0] + rt[:, _RT_C2:_RT_C2 + 1] * ybuf[slot, 1]
    o_ref[...] = y * lax.rsqrt(jnp.mean(y * y, axis=-1, keepdims=True) + RMS_EPS) * g_ref[...]


def _combine(x_all, rt, g, ys, pos, *, tok0, count, tm):
    n_tok, d = x_all.shape
    b0 = tok0 // tm
    grid_spec = pltpu.PrefetchScalarGridSpec(
        num_scalar_prefetch=1,
        grid=(count // tm,),
        in_specs=[pl.BlockSpec((tm, d), lambda i, pos: (b0 + i, 0)),
                  pl.BlockSpec((tm, LANES), lambda i, pos: (b0 + i, 0)),
                  pl.BlockSpec((1, d), lambda i, pos: (0, 0)),
                  pl.BlockSpec(memory_space=pl.ANY)],
        out_specs=pl.BlockSpec((tm, d), lambda i, pos: (i, 0)),
        scratch_shapes=[pltpu.VMEM((2, 2, tm, d), F32), pltpu.SemaphoreType.DMA((2,))],
    )
    return pl.pallas_call(
        functools.partial(_combine_kernel, tok0=tok0, n_tok=n_tok),
        out_shape=jax.ShapeDtypeStruct((count, d), F32),
        grid_spec=grid_spec,
        compiler_params=_params("arbitrary"),
        name="moe_combine",
    )(pos.reshape(-1), x_all, rt, g.reshape(1, d), ys)


def kernel(x_prompt, x_sample, mem_prompt, cache_mem_k, cache_mem_v, state_hgrn, state_rwkv, state_rwkv_shift, g_mix, w_in, hg_lb_logits, hg_norm_g, rw_mu, rw_w0, rw_w2, rw_a0, rw_a2, rw_g2, rw_kk, rw_ka, rw_rk, rw_ln_g, rw_ln_b, w_br_hg, w_br_rw, w_out, g_ca, g_mem, w_ca_q, w_ca_k, w_ca_v, w_ca_o, g_moe, w_rg, b_rg, w_re, b_re, w_e1, w_e3, w_e2, g_final):
    bp, seq, d = x_prompt.shape
    ns = x_sample.shape[0]
    tp = bp * seq
    t_all = tp + ns
    assert w_in.shape[0] == 1, "the final RMSNorm is fused into the last layer's MoE combine"
    layer = 0
    hg_w = HG_HEADS * HG_DIM
    rw_w = RW_HEADS * RW_DIM
    shift_w = state_rwkv_shift.shape[-1]
    rw_col = 4 * hg_w
    gate_col = rw_col + shift_w
    mem_t = mem_prompt.shape[1]
    tm = 640
    assert t_all % tm == 0

    lb = jnp.cumsum(jax.nn.softmax(hg_lb_logits.astype(F32), axis=0), axis=0)[layer]
    x_all = jnp.concatenate([x_prompt.reshape(tp, d), x_sample.reshape(ns, d)])

    h = _rmsnorm(x_all, g_mix[layer], BF16, tm)
    p_all = _matmul(h, w_in[layer], tm=tm, tn=1280, out_dtype=F32)
    gn = hg_norm_g[layer]
    o_hg_p, hg_p = _hgrn_prompt(p_all, lb, gn, batch=bp, seq=seq, col0=0, lblk=512)
    o_hg_s, hg_s = _hgrn_step(p_all, state_hgrn.reshape(state_hgrn.shape[1:]), lb, gn, row0=tp, col0=0)
    P = {'rw_w0': rw_w0[layer], 'rw_a0': rw_a0[layer], 'rw_kk': rw_kk[layer], 'rw_ka': rw_ka[layer],
         'rw_rk': rw_rk[layer], 'rw_ln_g': rw_ln_g[layer], 'rw_ln_b': rw_ln_b[layer],
         'rw_w2': rw_w2[layer], 'rw_a2': rw_a2[layer], 'rw_g2': rw_g2[layer]}
    vec, lora = _rw_param_blocks(P)
    mu = rw_mu[layer]
    o_rw_p, rw_p = _rwkv_prompt(p_all, jnp.zeros((bp, shift_w), F32), mu, vec, lora,
                                batch=bp, seq=seq, col0=rw_col // LANES, lblk=512)
    r_s, k_s, v_s, kk_s, kka_s, w_s, g_s, bonus_s = _rwkv_step_prep(
        p_all, state_rwkv_shift.reshape(ns, shift_w), mu, vec, lora, row0=tp, col0=rw_col // LANES)
    y_s, rw_s = _rwkv_step(state_rwkv.reshape(state_rwkv.shape[1:]), r_s, k_s, kk_s, kka_s, w_s, v_s)
    o_rw_s = _rwkv_step_post(y_s, bonus_s, g_s, vec)
    shift_p = p_all[seq - 1:tp:seq, rw_col:gate_col]
    shift_s = p_all[tp:, rw_col:gate_col]
    merged = _merge(jnp.concatenate([o_hg_p, o_hg_s]), jnp.concatenate([o_rw_p, o_rw_s]),
                    w_br_hg[layer], w_br_rw[layer], p_all, gate_col0=gate_col, tm=tm, tn=256)
    x1 = _matmul(merged, w_out[layer], tm=tm, tn=512, out_dtype=F32, res=x_all)

    q = _matmul(_rmsnorm(x1, g_ca[layer], BF16, tm), w_ca_q[layer], tm=tm, tn=512, out_dtype=BF16)
    m = _rmsnorm(mem_prompt.reshape(bp * mem_t, d), g_mem[layer], BF16, mem_t)
    mem_k = _matmul(m, w_ca_k[layer], tm=bp * mem_t, tn=512, out_dtype=F32)
    mem_v = _matmul(m, w_ca_v[layer], tm=bp * mem_t, tn=512, out_dtype=F32)
    o_p = _attn_prompt(q, mem_k.reshape(bp, mem_t, d), mem_v.reshape(bp, mem_t, d), batch=bp, seq=seq, tq=512)
    o_s = _attn_step(q[tp:], cache_mem_k.reshape(ns, mem_t, d).astype(BF16),
                     cache_mem_v.reshape(ns, mem_t, d).astype(BF16))
    x2 = _matmul(jnp.concatenate([o_p, o_s]), w_ca_o[layer], tm=tm, tn=512, out_dtype=F32, res=x1)

    h3, rt = _router(x2, g_moe[layer], w_rg[layer], b_rg[layer], w_re[layer], b_re[layer], tm=tm)
    n_tiles = (2 * t_all + MOE_EXPERTS * (MOE_TILE - 1)) // MOE_TILE
    row_src, tile_expert, n_used, pos = _dispatch_plan(
        rt[:, _RT_E1].astype(jnp.int32), rt[:, _RT_E2].astype(jnp.int32), n_tiles)
    ys = _experts(h3, w_e1[layer], w_e3[layer], w_e2[layer], row_src, tile_expert, n_used)
    y_p = _combine(x2, rt, g_final, ys, pos, tok0=0, count=tp, tm=LANES)
    y_s = _combine(x2, rt, g_final, ys, pos, tok0=tp, count=ns, tm=LANES)

    ca_heads_shape = (1, bp, mem_t, CA_HEADS, d // CA_HEADS)
    return (y_p.reshape(bp, seq, d), y_s.reshape(ns, 1, d), hg_p[None], rw_p[None], shift_p[None],
            mem_k.reshape(ca_heads_shape), mem_v.reshape(ca_heads_shape), hg_s[None], rw_s[None], shift_s[None])
```

```python
import functools

import jax
import jax.numpy as jnp
from jax import lax
from jax.experimental import pallas as pl
from jax.experimental.pallas import tpu as pltpu

F32 = jnp.float32
BF16 = jnp.bfloat16

RMS_EPS = 1e-6
GN_EPS = 64e-5

LANES = 128
VMEM_LIMIT_BYTES = 56 * 1024 * 1024

HG_HEADS = 8
HG_DIM = 128
HG_CHUNK = 64
RW_HEADS = 16
RW_DIM = 64
RW_CHUNK = 64
RW_SUB = 16
CA_HEADS = 4
MOE_GROUPS = 4
MOE_PER_GROUP = 8
MOE_EXPERTS = MOE_GROUPS * MOE_PER_GROUP
MOE_TILE = 256


def _params(*sem, vmem=None):
    return pltpu.CompilerParams(dimension_semantics=sem, vmem_limit_bytes=vmem)


def _nt(a, b):
    return lax.dot_general(a, b, (((1,), (1,)), ((), ())), preferred_element_type=F32)


def _tn(a, b):
    return lax.dot_general(a, b, (((0,), (0,)), ((), ())), preferred_element_type=F32)


def _nn(a, b):
    return jnp.dot(a, b, preferred_element_type=F32)


def _split2(x):
    hi = x.astype(BF16)
    lo = (x - hi.astype(F32)).astype(BF16)
    return hi, lo


def _split3(x):
    hi = x.astype(BF16)
    r = x - hi.astype(F32)
    mid = r.astype(BF16)
    lo = (r - mid.astype(F32)).astype(BF16)
    return hi, mid, lo


def _dot3(a, b, dot=_nn):
    ah, al = _split2(a)
    bh, bl = _split2(b)
    return dot(ah, bh) + (dot(ah, bl) + dot(al, bh))


def _dot1(a, b):
    return _nn(a.astype(BF16), b.astype(BF16))


def _dot_exact_rhs(a, b_bf16, dot=_nn):
    h, m, l = _split3(a)
    return dot(h, b_bf16) + (dot(m, b_bf16) + dot(l, b_bf16))


def _iota2(shape, dim):
    return lax.broadcasted_iota(jnp.int32, shape, dim)


def _silu(x):
    return x * jax.nn.sigmoid(x)


def _rms_kernel(x_ref, g_ref, o_ref):
    x = x_ref[...]
    ms = jnp.mean(x * x, axis=-1, keepdims=True)
    o_ref[...] = (x * lax.rsqrt(ms + RMS_EPS) * g_ref[...]).astype(o_ref.dtype)


def _rmsnorm(x, g, out_dtype, tm):
    t, d = x.shape
    return pl.pallas_call(
        _rms_kernel,
        out_shape=jax.ShapeDtypeStruct((t, d), out_dtype),
        grid=(t // tm,),
        in_specs=[pl.BlockSpec((tm, d), lambda i: (i, 0)), pl.BlockSpec((1, d), lambda i: (0, 0))],
        out_specs=pl.BlockSpec((tm, d), lambda i: (i, 0)),
        compiler_params=_params("parallel"),
        name="rmsnorm",
    )(x, g.reshape(1, d))


def _mm_kernel(x_ref, w_ref, *rest, has_res):
    if has_res:
        r_ref, o_ref, wb_ref = rest
    else:
        o_ref, wb_ref = rest

    @pl.when(pl.program_id(1) == 0)
    def _():
        wb_ref[...] = w_ref[...].astype(BF16)

    acc = _nn(x_ref[...], wb_ref[...])
    if has_res:
        acc = r_ref[...] + acc
    o_ref[...] = acc.astype(o_ref.dtype)


def _matmul(x, w, *, tm, tn, out_dtype, res=None):
    m, k = x.shape
    n = w.shape[1]
    in_specs = [pl.BlockSpec((tm, k), lambda j, i: (i, 0)), pl.BlockSpec((k, tn), lambda j, i: (0, j))]
    args = [x, w]
    if res is not None:
        in_specs.append(pl.BlockSpec((tm, tn), lambda j, i: (i, j)))
        args.append(res)
    return pl.pallas_call(
        functools.partial(_mm_kernel, has_res=res is not None),
        out_shape=jax.ShapeDtypeStruct((m, n), out_dtype),
        grid=(n // tn, m // tm),
        in_specs=in_specs,
        out_specs=pl.BlockSpec((tm, tn), lambda j, i: (i, j)),
        scratch_shapes=[pltpu.VMEM((k, tn), BF16)],
        compiler_params=_params("parallel", "arbitrary", vmem=VMEM_LIMIT_BYTES),
        name="matmul",
    )(*args)


def _hgrn_gates(hq, hf, lb):
    q = _silu(hq)
    f = lb + (1.0 - lb) * jax.nn.sigmoid(hf)
    return q, f


def _hgrn_out(o, gn, hog):
    ms = jnp.mean(o * o, axis=-1, keepdims=True)
    return o * lax.rsqrt(ms + RMS_EPS) * gn * _silu(hog)


def _hgrn_prompt_kernel(q_ref, f_ref, i_ref, og_ref, lb_ref, gn_ref, o_ref, s_ref, st_ref, *, n_chunks):
    l = pl.program_id(2)
    c = HG_CHUNK

    @pl.when(l == 0)
    def _():
        st_ref[...] = jnp.zeros_like(st_ref)

    lb = lb_ref[...]
    gn = gn_ref[...]
    incl = _iota2((c, c), 0) >= _iota2((c, c), 1)
    tri = incl.astype(BF16)
    for ci in range(n_chunks):
        sl = slice(ci * c, (ci + 1) * c)
        q, f = _hgrn_gates(q_ref[sl, :], f_ref[sl, :], lb)
        k = 1.0 - f
        vb = i_ref[sl, :].astype(BF16)
        b = _cumsum_rows(jnp.log(f), tri)
        b_last = b[c - 1:c, :]
        q_in = (q * jnp.exp(b)).astype(BF16)
        k_in = (k * jnp.exp(-b)).astype(BF16)
        k_out = (k * jnp.exp(b_last - b)).astype(BF16)
        st = st_ref[...]
        o_inter = _nt(q_in, st.astype(BF16))
        att = jnp.where(incl, _nt(q_in, k_in), 0.0)
        o_intra = _nn(att.astype(BF16), vb)
        st_ref[...] = st * jnp.exp(b_last) + _tn(vb, k_out)
        o_ref[sl, :] = _hgrn_out(o_inter + o_intra, gn, og_ref[sl, :]).astype(o_ref.dtype)

    @pl.when(l == pl.num_programs(2) - 1)
    def _():
        s_ref[0, 0] = st_ref[...].T


def _cumsum_rows(x, tri_bf16):
    h, m, l = _split3(x)
    return _nn(tri_bf16, h) + (_nn(tri_bf16, m) + _nn(tri_bf16, l))


def _hgrn_prompt(p_all, lb, gn, *, batch, seq, col0, lblk):
    nl = seq // lblk
    hw = HG_HEADS * HG_DIM
    nb = hw // LANES

    def pspec(seg):
        return pl.BlockSpec((lblk, LANES), lambda b, h, l: (b * nl + l, col0 + seg * nb + h))

    vec = pl.BlockSpec((1, LANES), lambda b, h, l: (0, h))
    return pl.pallas_call(
        functools.partial(_hgrn_prompt_kernel, n_chunks=lblk // HG_CHUNK),
        out_shape=(jax.ShapeDtypeStruct((batch * seq, hw), BF16),
                   jax.ShapeDtypeStruct((batch, HG_HEADS, HG_DIM, HG_DIM), F32)),
        grid=(batch, HG_HEADS, nl),
        in_specs=[pspec(0), pspec(1), pspec(2), pspec(3), vec, vec],
        out_specs=(pl.BlockSpec((lblk, LANES), lambda b, h, l: (b * nl + l, h)),
                   pl.BlockSpec((1, 1, HG_DIM, HG_DIM), lambda b, h, l: (b, h, 0, 0))),
        scratch_shapes=[pltpu.VMEM((HG_DIM, HG_DIM), F32)],
        compiler_params=_params("parallel", "parallel", "arbitrary"),
        name="hgrn_prompt",
    )(p_all, p_all, p_all, p_all, lb.reshape(1, hw), gn.reshape(1, hw))


_RW_W0, _RW_A0, _RW_KK, _RW_KA, _RW_RK, _RW_LNG, _RW_LNB = range(7)


def _head_ones():
    same = (_iota2((LANES, LANES), 0) // RW_DIM) == (_iota2((LANES, LANES), 1) // RW_DIM)
    return same.astype(BF16)


def _head_sum(x, ones_bd):
    return _dot_exact_rhs(x, ones_bd)


def _rw_prep(xr, xk, xv, xc, vec, lora, ones_bd):
    c1 = xc[:, :LANES]
    w_lora = _nn(jnp.tanh(c1).astype(BF16), lora[0].astype(BF16))
    a_lora = _nn(c1.astype(BF16), lora[1].astype(BF16))
    g = _nn(jax.nn.sigmoid(xc[:, LANES:]).astype(BF16), lora[2].astype(BF16))
    z = -(vec[_RW_W0:_RW_W0 + 1] + w_lora)
    softplus = jnp.maximum(z, 0.0) + jnp.log1p(jnp.exp(-jnp.abs(z)))
    lw = -jnp.exp(-softplus - 0.5)
    a = jax.nn.sigmoid(vec[_RW_A0:_RW_A0 + 1] + a_lora)
    kk = xk * vec[_RW_KK:_RW_KK + 1]
    kk = kk / jnp.maximum(jnp.sqrt(_head_sum(kk * kk, ones_bd)), 1e-12)
    k = xk * (1.0 + (a - 1.0) * vec[_RW_KA:_RW_KA + 1])
    bonus = _head_sum(xr * k * vec[_RW_RK:_RW_RK + 1], ones_bd) * xv
    return xr, k, xv, kk, kk * a, lw, g, bonus


def _rw_post(y, bonus, g, vec, ones_bd):
    inv = 1.0 / RW_DIM
    mu = _head_sum(y, ones_bd) * inv
    d = y - mu
    var = _head_sum(d * d, ones_bd) * inv
    yn = d * lax.rsqrt(var + GN_EPS) * vec[_RW_LNG:_RW_LNG + 1] + vec[_RW_LNB:_RW_LNB + 1]
    return (yn + bonus) * g


def _rw_chunk_operands(r, k, v, kk, kka, lw, tri):
    c = r.shape[0]
    ginc = _cumsum_rows(lw, tri)
    glast = ginc[c - 1:c, :]
    e_neg = jnp.exp(-ginc)
    e_rem = jnp.exp(glast - ginc)
    a_hat = kk * jnp.exp(ginc - lw)
    r_hat = r * jnp.exp(ginc)
    e_last = jnp.exp(glast)
    b_hat, k_hat, b_til, k_til = kka * e_neg, k * e_neg, kka * e_rem, k * e_rem
    units = []
    for h in range(LANES // RW_DIM):
        s = slice(h * RW_DIM, (h + 1) * RW_DIM)
        units.append(dict(
            ah=a_hat[:, s], rh=r_hat[:, s], ab=a_hat[:, s].astype(BF16), rb=r_hat[:, s].astype(BF16),
            bb=b_hat[:, s].astype(BF16), kb=k_hat[:, s].astype(BF16), bt=b_til[:, s].astype(BF16),
            kt=k_til[:, s].astype(BF16), vb=v[:, s].astype(BF16), e_last=e_last[:, s]))
    return units


def _rw_local_levels(us, out):
    c = us[0]['ab'].shape[0]
    row, col = _iota2((c, c), 0), _iota2((c, c), 1)
    incl, strict = row >= col, row > col
    diag_blk = (row // RW_SUB) == (col // RW_SUB)
    eye_c = (row == col).astype(F32)
    eye_d = _iota2((RW_DIM, RW_DIM), 0) == _iota2((RW_DIM, RW_DIM), 1)

    lm = [jnp.where(strict, _nt(u['ab'], u['bb']), 0.0) for u in us]
    lk = [jnp.where(strict, _nt(u['ab'], u['kb']), 0.0).astype(BF16) for u in us]
    pb = [jnp.where(incl, _nt(u['rb'], u['bb']), 0.0).astype(BF16) for u in us]
    pk = [jnp.where(incl, _nt(u['rb'], u['kb']), 0.0).astype(BF16) for u in us]
    yield
    pw = [jnp.where(diag_blk, -x, 0.0) for x in lm]
    e = [jnp.where(diag_blk, 0.0, x) for x in lm]
    t_d = [eye_c + x for x in pw]
    x = [jnp.concatenate([u['ah'], _nn(lk_, u['vb'])], axis=1) for u, lk_ in zip(us, lk)]
    for _ in range(RW_SUB.bit_length() - 2):
        pw = [_dot1(p, p) for p in pw]
        yield
        t_d = [t + _dot1(t, p) for t, p in zip(t_d, pw)]
    yield
    m = [_dot1(t, e_) for t, e_ in zip(t_d, e)]
    tdx = [_dot1(t, x_) for t, x_ in zip(t_d, x)]
    yield
    acc = [eye_c - m_ for m_ in m]
    pw = m
    for _ in range((c // RW_SUB).bit_length() - 2):
        pw = [_dot1(p, p) for p in pw]
        yield
        acc = [a + _dot1(a, p) for a, p in zip(acc, pw)]
        yield
    txb = [_dot1(a, t).astype(BF16) for a, t in zip(acc, tdx)]
    yield
    p_au = [_nn(p, t) for p, t in zip(pb, txb)]
    b_au = [_tn(u['bt'], t) for u, t in zip(us, txb)]
    pkv = [_nn(p, u['vb']) for u, p in zip(us, pk)]
    ktv = [_tn(u['kt'], u['vb']) for u in us]
    yield
    for i, u in enumerate(us):
        out.append((u['rh'] - p_au[i][:, :RW_DIM], pkv[i] - p_au[i][:, RW_DIM:],
                    jnp.where(eye_d, u['e_last'], 0.0) - b_au[i][:, :RW_DIM], ktv[i] - b_au[i][:, RW_DIM:]))


RW_GROUP_CHUNKS = 4


def _rwkv_prompt_kernel(pr_ref, pk_ref, pv_ref, pc_ref, sr_ref, sk_ref, sv_ref, sc_ref,
                        mr_ref, mk_ref, mv_ref, mc_ref, vec_ref, lora_ref,
                        o_ref, s_ref, a_ref, y_ref, cr_ref, ck_ref, cv_ref, cc_ref, *, n_chunks):
    l = pl.program_id(2)
    lblk = pr_ref.shape[0]
    c = RW_CHUNK

    @pl.when(l == 0)
    def _():
        a_ref[...] = jnp.zeros_like(a_ref)
        cr_ref[...] = sr_ref[0]
        ck_ref[...] = sk_ref[0]
        cv_ref[...] = sv_ref[0]
        cc_ref[...] = sc_ref[0]

    def shift_mix(x_ref, carry_ref, mu_ref):
        x = x_ref[...]
        first = _iota2(x.shape, 0) == 0
        prev = jnp.where(first, carry_ref[...], pltpu.roll(x, 1, 0))
        carry_ref[...] = x[lblk - 1:lblk, :]
        return x + mu_ref[...] * (prev - x)

    ones_bd = _head_ones()
    vec = vec_ref[...]
    r, k, v, kk, kka, lw, g, bonus = _rw_prep(
        shift_mix(pr_ref, cr_ref, mr_ref), shift_mix(pk_ref, ck_ref, mk_ref), shift_mix(pv_ref, cv_ref, mv_ref),
        shift_mix(pc_ref, cc_ref, mc_ref), vec, lora_ref[...], ones_bd)
    tri = (_iota2((c, c), 0) >= _iota2((c, c), 1)).astype(BF16)
    n_heads = LANES // RW_DIM
    state = [a_ref[h] for h in range(n_heads)]

    def state_step(ci, local):
        ys = []
        for h in range(n_heads):
            r_til, y_loc, g_mat, a_loc = local[h]
            ys.append(_dot3(r_til, state[h]) + y_loc)
            state[h] = _dot3(g_mat, state[h]) + a_loc
        y_ref[ci * c:(ci + 1) * c, :] = jnp.concatenate(ys, axis=1)

    pending = []
    for g0 in range(0, n_chunks, RW_GROUP_CHUNKS):
        chunks = range(g0, min(g0 + RW_GROUP_CHUNKS, n_chunks))
        units = []
        for ci in chunks:
            sl = slice(ci * c, (ci + 1) * c)
            units += _rw_chunk_operands(r[sl], k[sl], v[sl], kk[sl], kka[sl], lw[sl], tri)
        local = []
        for _ in _rw_local_levels(units, local):
            if pending:
                pending.pop(0)()
        pending += [functools.partial(state_step, ci, local[i * n_heads:(i + 1) * n_heads])
                    for i, ci in enumerate(chunks)]
    for step in pending:
        step()
    for h in range(n_heads):
        a_ref[h] = state[h]
    o_ref[...] = _rw_post(y_ref[...], bonus, g, vec, ones_bd).astype(o_ref.dtype)

    @pl.when(l == pl.num_programs(2) - 1)
    def _():
        for h in range(n_heads):
            s_ref[0, h] = state[h].T


def _rw_param_blocks(P):
    rw_w = RW_HEADS * RW_DIM
    vec = jnp.stack([P['rw_w0'], P['rw_a0'], P['rw_kk'], P['rw_ka'], P['rw_rk'], P['rw_ln_g'], P['rw_ln_b'],
                     jnp.zeros((rw_w,), F32)])
    z = jnp.zeros_like(P['rw_w2'])
    lora = jnp.stack([jnp.concatenate([P['rw_w2'], z]), jnp.concatenate([z, P['rw_a2']]), P['rw_g2']])
    return vec, lora


def _rwkv_prompt(p_all, shift0, mu, vec, lora, *, batch, seq, col0, lblk):
    nl = seq // lblk
    rw_w = RW_HEADS * RW_DIM
    nb = rw_w // LANES
    code_blk = (col0 + 3 * nb) // 2
    shift_w = shift0.shape[-1]

    def pspec(seg):
        return pl.BlockSpec((lblk, LANES), lambda b, h, l: (b * nl + l, col0 + seg * nb + h))

    def sspec(seg):
        return pl.BlockSpec((1, 1, LANES), lambda b, h, l: (b, 0, seg * nb + h))

    def mspec(seg):
        return pl.BlockSpec((1, LANES), lambda b, h, l: (0, seg * nb + h))

    in_specs = [pspec(0), pspec(1), pspec(2),
                pl.BlockSpec((lblk, 2 * LANES), lambda b, h, l: (b * nl + l, code_blk)),
                sspec(0), sspec(1), sspec(2),
                pl.BlockSpec((1, 1, 2 * LANES), lambda b, h, l: (b, 0, 3 * nb // 2)),
                mspec(0), mspec(1), mspec(2),
                pl.BlockSpec((1, 2 * LANES), lambda b, h, l: (0, 3 * nb // 2)),
                pl.BlockSpec((8, LANES), lambda b, h, l: (0, h)),
                pl.BlockSpec((3, LANES, LANES), lambda b, h, l: (0, 0, h))]
    sh3 = shift0.reshape(batch, 1, shift_w)
    mu2 = mu.reshape(1, shift_w)
    return pl.pallas_call(
        functools.partial(_rwkv_prompt_kernel, n_chunks=lblk // RW_CHUNK),
        out_shape=(jax.ShapeDtypeStruct((batch * seq, rw_w), BF16),
                   jax.ShapeDtypeStruct((batch, RW_HEADS, RW_DIM, RW_DIM), F32)),
        grid=(batch, nb, nl),
        in_specs=in_specs,
        out_specs=(pl.BlockSpec((lblk, LANES), lambda b, h, l: (b * nl + l, h)),
                   pl.BlockSpec((1, 2, RW_DIM, RW_DIM), lambda b, h, l: (b, h, 0, 0))),
        scratch_shapes=[pltpu.VMEM((2, RW_DIM, RW_DIM), F32), pltpu.VMEM((lblk, LANES), F32),
                        pltpu.VMEM((1, LANES), F32), pltpu.VMEM((1, LANES), F32), pltpu.VMEM((1, LANES), F32),
                        pltpu.VMEM((1, 2 * LANES), F32)],
        compiler_params=_params("parallel", "parallel", "arbitrary"),
        name="rwkv_prompt",
    )(p_all, p_all, p_all, p_all, sh3, sh3, sh3, sh3, mu2, mu2, mu2, mu2, vec, lora)


STEP_REQ = 16


def _transpose_rows(x):
    r = x.shape[0]
    eye = (_iota2((r, r), 0) == _iota2((r, r), 1)).astype(BF16)
    return _dot_exact_rhs(x, eye, dot=_tn)


def _hgrn_step_kernel(q_ref, f_ref, i_ref, og_ref, lb_ref, gn_ref, s_ref, o_ref, so_ref):
    q, f = _hgrn_gates(q_ref[...], f_ref[...], lb_ref[...])
    f_t = _transpose_rows(f)
    q_t = _transpose_rows(q)
    v = i_ref[...]
    rows = []
    for j in range(STEP_REQ):
        fc = f_t[:, j:j + 1]
        s_new = s_ref[j, 0] * fc + (1.0 - fc) * v[j:j + 1, :]
        so_ref[j, 0] = s_new
        rows.append(jnp.sum(s_new * q_t[:, j:j + 1], axis=0, keepdims=True))
    o = jnp.concatenate(rows, axis=0)
    o_ref[...] = _hgrn_out(o, gn_ref[...], og_ref[...]).astype(o_ref.dtype)


def _hgrn_step(p_all, state, lb, gn, *, row0, col0):
    n = state.shape[0]
    hw = HG_HEADS * HG_DIM
    nb = hw // LANES
    rb0 = row0 // STEP_REQ

    def pspec(seg):
        return pl.BlockSpec((STEP_REQ, LANES), lambda i, h: (rb0 + i, col0 + seg * nb + h))

    vec = pl.BlockSpec((1, LANES), lambda i, h: (0, h))
    sspec = pl.BlockSpec((STEP_REQ, 1, HG_DIM, HG_DIM), lambda i, h: (i, h, 0, 0))
    return pl.pallas_call(
        _hgrn_step_kernel,
        out_shape=(jax.ShapeDtypeStruct((n, hw), BF16), jax.ShapeDtypeStruct(state.shape, F32)),
        grid=(n // STEP_REQ, HG_HEADS),
        in_specs=[pspec(0), pspec(1), pspec(2), pspec(3), vec, vec, sspec],
        out_specs=(pl.BlockSpec((STEP_REQ, LANES), lambda i, h: (i, h)), sspec),
        compiler_params=_params("parallel", "parallel"),
        name="hgrn_step",
    )(p_all, p_all, p_all, p_all, lb.reshape(1, hw), gn.reshape(1, hw), state)


def _rwkv_step_prep_kernel(pr_ref, pk_ref, pv_ref, pc_ref, sr_ref, sk_ref, sv_ref, sc_ref,
                           mr_ref, mk_ref, mv_ref, mc_ref, vec_ref, lora_ref,
                           r_ref, k_ref, v_ref, kk_ref, kka_ref, w_ref, g_ref, bonus_ref):
    def mix(x_ref, prev_ref, mu_ref):
        x = x_ref[...]
        return x + mu_ref[...] * (prev_ref[...] - x)

    r, k, v, kk, kka, lw, g, bonus = _rw_prep(
        mix(pr_ref, sr_ref, mr_ref), mix(pk_ref, sk_ref, mk_ref), mix(pv_ref, sv_ref, mv_ref),
        mix(pc_ref, sc_ref, mc_ref), vec_ref[...], lora_ref[...], _head_ones())
    r_ref[...] = r
    k_ref[...] = k
    v_ref[...] = v
    kk_ref[...] = kk
    kka_ref[...] = kka
    w_ref[...] = jnp.exp(lw)
    g_ref[...] = g
    bonus_ref[...] = bonus


def _rwkv_step_prep(p_all, shift, mu, vec, lora, *, row0, col0):
    n, shift_w = shift.shape
    rw_w = RW_HEADS * RW_DIM
    nb = rw_w // LANES
    rb0 = row0 // n
    code_blk = (col0 + 3 * nb) // 2

    def pspec(seg):
        return pl.BlockSpec((n, LANES), lambda h: (rb0, col0 + seg * nb + h))

    def sspec(seg):
        return pl.BlockSpec((n, LANES), lambda h: (0, seg * nb + h))

    def mspec(seg):
        return pl.BlockSpec((1, LANES), lambda h: (0, seg * nb + h))

    ospec = pl.BlockSpec((n, LANES), lambda h: (0, h))
    mu2 = mu.reshape(1, shift_w)
    return pl.pallas_call(
        _rwkv_step_prep_kernel,
        out_shape=tuple(jax.ShapeDtypeStruct((n, rw_w), F32) for _ in range(8)),
        grid=(nb,),
        in_specs=[pspec(0), pspec(1), pspec(2), pl.BlockSpec((n, 2 * LANES), lambda h: (rb0, code_blk)),
                  sspec(0), sspec(1), sspec(2), pl.BlockSpec((n, 2 * LANES), lambda h: (0, 3 * nb // 2)),
                  mspec(0), mspec(1), mspec(2), pl.BlockSpec((1, 2 * LANES), lambda h: (0, 3 * nb // 2)),
                  pl.BlockSpec((8, LANES), lambda h: (0, h)), pl.BlockSpec((3, LANES, LANES), lambda h: (0, 0, h))],
        out_specs=tuple(ospec for _ in range(8)),
        compiler_params=_params("parallel"),
        name="rwkv_step_prep",
    )(p_all, p_all, p_all, p_all, shift, shift, shift, shift, mu2, mu2, mu2, mu2, vec, lora)


def _rwkv_step_kernel(s_ref, r_ref, k_ref, kk_ref, kka_ref, w_ref, vt_ref, so_ref, yt_ref):
    def head(h, carry):
        rows = pl.ds(pl.multiple_of(h * RW_DIM, RW_DIM), RW_DIM)
        for j in range(STEP_REQ):
            s = s_ref[j, h]
            sa = jnp.sum(s * kk_ref[j, pl.ds(h, 1), :], axis=1, keepdims=True)
            s_new = (s * w_ref[j, pl.ds(h, 1), :] - sa * kka_ref[j, pl.ds(h, 1), :]
                     + vt_ref[0, rows, j:j + 1] * k_ref[j, pl.ds(h, 1), :])
            so_ref[j, h] = s_new
            yt_ref[0, rows, j:j + 1] = jnp.sum(s_new * r_ref[j, pl.ds(h, 1), :], axis=1, keepdims=True)
        return carry

    lax.fori_loop(0, RW_HEADS, head, 0)


def _rwkv_step(state, r, k, kk, kka, w, v):
    n = state.shape[0]
    rw_w = RW_HEADS * RW_DIM
    nblk = n // STEP_REQ
    heads = lambda t: t.reshape(n, RW_HEADS, RW_DIM)
    v_t = v.reshape(nblk, STEP_REQ, rw_w).transpose(0, 2, 1)
    sspec = pl.BlockSpec((STEP_REQ, RW_HEADS, RW_DIM, RW_DIM), lambda i: (i, 0, 0, 0))
    rspec = pl.BlockSpec((STEP_REQ, RW_HEADS, RW_DIM), lambda i: (i, 0, 0))
    cspec = pl.BlockSpec((1, rw_w, STEP_REQ), lambda i: (i, 0, 0))
    s_new, y_t = pl.pallas_call(
        _rwkv_step_kernel,
        out_shape=(jax.ShapeDtypeStruct(state.shape, F32), jax.ShapeDtypeStruct((nblk, rw_w, STEP_REQ), F32)),
        grid=(nblk,),
        in_specs=[sspec, rspec, rspec, rspec, rspec, rspec, cspec],
        out_specs=(sspec, cspec),
        compiler_params=_params("parallel", vmem=VMEM_LIMIT_BYTES),
        name="rwkv_step",
    )(state, heads(r), heads(k), heads(kk), heads(kka), heads(w), v_t)
    return y_t.transpose(0, 2, 1).reshape(n, rw_w), s_new


def _rwkv_step_post_kernel(y_ref, bonus_ref, g_ref, vec_ref, o_ref):
    o_ref[...] = _rw_post(y_ref[...], bonus_ref[...], g_ref[...], vec_ref[...], _head_ones()).astype(o_ref.dtype)


def _rwkv_step_post(y, bonus, g, vec):
    n, rw_w = y.shape
    spec = pl.BlockSpec((n, LANES), lambda h: (0, h))
    return pl.pallas_call(
        _rwkv_step_post_kernel,
        out_shape=jax.ShapeDtypeStruct((n, rw_w), BF16),
        grid=(rw_w // LANES,),
        in_specs=[spec, spec, spec, pl.BlockSpec((8, LANES), lambda h: (0, h))],
        out_specs=spec,
        compiler_params=_params("parallel"),
        name="rwkv_step_post",
    )(y, bonus, g, vec)


def _merge_kernel(oh_ref, or_ref, wh_ref, wr_ref, gh_ref, gr_ref, o_ref, whb_ref, wrb_ref):
    @pl.when(pl.program_id(1) == 0)
    def _():
        whb_ref[...] = wh_ref[...].astype(BF16)
        wrb_ref[...] = wr_ref[...].astype(BF16)

    m = (jax.nn.sigmoid(gh_ref[...]) * _nn(oh_ref[...], whb_ref[...])
         + jax.nn.sigmoid(gr_ref[...]) * _nn(or_ref[...], wrb_ref[...]))
    o_ref[...] = m.astype(o_ref.dtype)


def _merge(o_hg, o_rw, w_hg, w_rw, p_all, *, gate_col0, tm, tn):
    m, kh = o_hg.shape
    kr = o_rw.shape[1]
    n = w_hg.shape[1]
    gb = gate_col0 // tn
    return pl.pallas_call(
        _merge_kernel,
        out_shape=jax.ShapeDtypeStruct((m, n), BF16),
        grid=(n // tn, m // tm),
        in_specs=[pl.BlockSpec((tm, kh), lambda j, i: (i, 0)), pl.BlockSpec((tm, kr), lambda j, i: (i, 0)),
                  pl.BlockSpec((kh, tn), lambda j, i: (0, j)), pl.BlockSpec((kr, tn), lambda j, i: (0, j)),
                  pl.BlockSpec((tm, tn), lambda j, i: (i, gb + j)),
                  pl.BlockSpec((tm, tn), lambda j, i: (i, gb + n // tn + j))],
        out_specs=pl.BlockSpec((tm, tn), lambda j, i: (i, j)),
        scratch_shapes=[pltpu.VMEM((kh, tn), BF16), pltpu.VMEM((kr, tn), BF16)],
        compiler_params=_params("parallel", "arbitrary"),
        name="merge",
    )(o_hg, o_rw, w_hg, w_rw, p_all, p_all)


def _softmax_rows(s):
    e = jnp.exp(s - jnp.max(s, axis=-1, keepdims=True))
    return e / jnp.sum(e, axis=-1, keepdims=True)


def _attn_heads(q, kb_ref, vb_ref, o_ref, store):
    dh = q.shape[-1] // CA_HEADS
    scale = dh ** -0.5
    for h in range(CA_HEADS):
        hs = slice(h * dh, (h + 1) * dh)
        p = _softmax_rows(_nt(q[:, hs], kb_ref[:, hs]) * scale)
        store(hs, _nn(p.astype(BF16), vb_ref[:, hs]))


def _attn_prompt_kernel(q_ref, k_ref, v_ref, o_ref, kb_ref, vb_ref):
    @pl.when(pl.program_id(1) == 0)
    def _():
        kb_ref[...] = k_ref[0].astype(BF16)
        vb_ref[...] = v_ref[0].astype(BF16)

    def store(hs, o):
        o_ref[:, hs] = o.astype(o_ref.dtype)

    _attn_heads(q_ref[...], kb_ref, vb_ref, o_ref, store)


def _attn_prompt(q_all, mem_k, mem_v, *, batch, seq, tq):
    _, mt, d = mem_k.shape
    nl = seq // tq
    kv = pl.BlockSpec((1, mt, d), lambda b, l: (b, 0, 0))
    qs = pl.BlockSpec((tq, d), lambda b, l: (b * nl + l, 0))
    return pl.pallas_call(
        _attn_prompt_kernel,
        out_shape=jax.ShapeDtypeStruct((batch * seq, d), BF16),
        grid=(batch, nl),
        in_specs=[qs, kv, kv],
        out_specs=qs,
        scratch_shapes=[pltpu.VMEM((mt, d), BF16), pltpu.VMEM((mt, d), BF16)],
        compiler_params=_params("parallel", "arbitrary"),
        name="attn_prompt",
    )(q_all, mem_k, mem_v)


def _attn_step_kernel(q_ref, k_ref, v_ref, o_ref, kb_ref, vb_ref):
    kb_ref[...] = k_ref[0].astype(BF16)
    vb_ref[...] = v_ref[0].astype(BF16)
    q = jnp.broadcast_to(q_ref[0], (8, q_ref.shape[-1]))

    def store(hs, o):
        o_ref[0, :, hs] = o[0:1].astype(o_ref.dtype)

    _attn_heads(q, kb_ref, vb_ref, o_ref, store)


def _attn_step(q, mem_k, mem_v):
    n, mt, d = mem_k.shape
    kv = pl.BlockSpec((1, mt, d), lambda b: (b, 0, 0))
    qs = pl.BlockSpec((1, 1, d), lambda b: (b, 0, 0))
    o = pl.pallas_call(
        _attn_step_kernel,
        out_shape=jax.ShapeDtypeStruct((n, 1, d), BF16),
        grid=(n,),
        in_specs=[qs, kv, kv],
        out_specs=qs,
        scratch_shapes=[pltpu.VMEM((mt, d), BF16), pltpu.VMEM((mt, d), BF16)],
        compiler_params=_params("parallel"),
        name="attn_step",
    )(q.reshape(n, 1, d), mem_k, mem_v)
    return o.reshape(n, d)


_RT_E1, _RT_E2, _RT_C1, _RT_C2 = range(4)


def _router_kernel(x_ref, g_ref, wr_ref, br_ref, h_ref, rt_ref):
    x = x_ref[...]
    h = x * lax.rsqrt(jnp.mean(x * x, axis=-1, keepdims=True) + RMS_EPS) * g_ref[...]
    h_ref[...] = h
    logits = _dot3(h, wr_ref[...]) + br_ref[...]
    lane = _iota2(logits.shape, 1)
    lane_f = lane.astype(F32)
    neg = -jnp.inf
    first = lambda hit: jnp.min(jnp.where(hit, lane_f, float(LANES)), axis=-1, keepdims=True)

    is_group = lane < MOE_GROUPS
    gl = jnp.where(is_group, logits, neg)
    g_max = jnp.max(gl, axis=-1, keepdims=True)
    g_top = first(gl == g_max)
    g_w = 1.0 / jnp.sum(jnp.where(is_group, jnp.exp(logits - g_max), 0.0), axis=-1, keepdims=True)

    e_idx = lane - MOE_GROUPS
    in_group = (e_idx >= 0) & (e_idx < MOE_EXPERTS) & (jnp.right_shift(e_idx, MOE_PER_GROUP.bit_length() - 1).astype(F32) == g_top)
    el = jnp.where(in_group, logits, neg)
    m1 = jnp.max(el, axis=-1, keepdims=True)
    i1 = first(el == m1)
    el2 = jnp.where(lane_f == i1, neg, el)
    m2 = jnp.max(el2, axis=-1, keepdims=True)
    i2 = first(el2 == m2)
    t = jnp.exp(m2 - m1)
    w1 = 1.0 / (1.0 + t)
    w2 = t / (1.0 + t)
    rec = jnp.where(lane == _RT_E1, i1 - MOE_GROUPS, 0.0)
    rec = jnp.where(lane == _RT_E2, i2 - MOE_GROUPS, rec)
    rec = jnp.where(lane == _RT_C1, g_w * w1, rec)
    rec = jnp.where(lane == _RT_C2, g_w * w2, rec)
    rt_ref[...] = rec


def _router(x, g, w_rg, b_rg, w_re, b_re, *, tm):
    t, d = x.shape
    pad = LANES - MOE_GROUPS - MOE_EXPERTS
    wr = jnp.concatenate([w_rg, w_re, jnp.zeros((d, pad), F32)], axis=1)
    br = jnp.concatenate([b_rg, b_re, jnp.zeros((pad,), F32)]).reshape(1, LANES)
    return pl.pallas_call(
        _router_kernel,
        out_shape=(jax.ShapeDtypeStruct((t, d), F32), jax.ShapeDtypeStruct((t, LANES), F32)),
        grid=(t // tm,),
        in_specs=[pl.BlockSpec((tm, d), lambda i: (i, 0)), pl.BlockSpec((1, d), lambda i: (0, 0)),
                  pl.BlockSpec((d, LANES), lambda i: (0, 0)), pl.BlockSpec((1, LANES), lambda i: (0, 0))],
        out_specs=(pl.BlockSpec((tm, d), lambda i: (i, 0)), pl.BlockSpec((tm, LANES), lambda i: (i, 0))),
        compiler_params=_params("parallel", vmem=VMEM_LIMIT_BYTES),
        name="moe_router",
    )(x, g.reshape(1, d), wr, br)


def _dispatch_plan(e1, e2, n_tiles):
    t = e1.shape[0]
    keys = jnp.concatenate([e1, e2])
    onehot = (keys[:, None] == jnp.arange(MOE_EXPERTS, dtype=jnp.int32)[None, :]).astype(jnp.int32)
    rank = jnp.take_along_axis(jnp.cumsum(onehot, axis=0), keys[:, None], axis=1)[:, 0] - 1
    counts = jnp.sum(onehot, axis=0)
    tiles = (counts + MOE_TILE - 1) // MOE_TILE
    tile_end = jnp.cumsum(tiles)
    tile_start = tile_end - tiles
    dest = tile_start[keys] * MOE_TILE + rank
    token = jnp.concatenate([jnp.arange(t, dtype=jnp.int32)] * 2)
    row_src = jnp.zeros((n_tiles * MOE_TILE,), jnp.int32).at[dest].set(token)
    n_used = tile_end[-1].astype(jnp.int32)
    tile_id = jnp.arange(n_tiles, dtype=jnp.int32)
    tile_expert = jnp.sum(tile_id[:, None] >= tile_end[None, :], axis=1).astype(jnp.int32)
    tile_expert = jnp.minimum(tile_expert, tile_expert[jnp.maximum(n_used - 1, 0)])
    return row_src, tile_expert, n_used.reshape(1), dest.reshape(2, t).astype(jnp.int32)


def _expert_kernel(te_ref, nu_ref, src_ref, h_hbm, w1_ref, w3_ref, w2_ref, o_ref,
                   xbuf, sem, w1b, w3b, w2b):
    i = pl.program_id(0)
    n_used = nu_ref[0]

    def gather(tile, slot):
        def body(r2, carry):
            for prio in range(2):
                r = 2 * r2 + prio
                row = src_ref[tile * MOE_TILE + r]
                pltpu.make_async_copy(h_hbm.at[pl.ds(row, 1)], xbuf.at[slot, pl.ds(r, 1)],
                                      sem.at[slot]).start(priority=prio)
            return carry
        lax.fori_loop(0, MOE_TILE // 2, body, 0)

    @pl.when(i == 0)
    def _():
        gather(0, 0)

    @pl.when(i + 1 < n_used)
    def _():
        gather(i + 1, (i + 1) % 2)

    @pl.when(i < n_used)
    def _():
        slot = i % 2
        pltpu.make_async_copy(h_hbm.at[pl.ds(0, MOE_TILE)], xbuf.at[slot], sem.at[slot]).wait()

        @pl.when((i == 0) | (te_ref[i] != te_ref[jnp.maximum(i - 1, 0)]))
        def _():
            w1b[...] = w1_ref[0].astype(BF16)
            w3b[...] = w3_ref[0].astype(BF16)
            w2b[...] = w2_ref[0].astype(BF16)

        x = xbuf[slot].astype(BF16)
        hid = _silu(_nn(x, w1b[...])) * _nn(x, w3b[...])
        o_ref[...] = _nn(hid.astype(BF16), w2b[...])

    @pl.when(i >= n_used)
    def _():
        o_ref[...] = jnp.zeros_like(o_ref)


def _experts(h, w1, w3, w2, row_src, tile_expert, n_used):
    t, d = h.shape
    f = w1.shape[-1]
    n_tiles = tile_expert.shape[0]
    grid_spec = pltpu.PrefetchScalarGridSpec(
        num_scalar_prefetch=3,
        grid=(n_tiles,),
        in_specs=[pl.BlockSpec(memory_space=pl.ANY),
                  pl.BlockSpec((1, d, f), lambda i, te, nu, src: (te[i], 0, 0)),
                  pl.BlockSpec((1, d, f), lambda i, te, nu, src: (te[i], 0, 0)),
                  pl.BlockSpec((1, f, d), lambda i, te, nu, src: (te[i], 0, 0))],
        out_specs=pl.BlockSpec((MOE_TILE, d), lambda i, te, nu, src: (i, 0)),
        scratch_shapes=[pltpu.VMEM((2, MOE_TILE, d), F32), pltpu.SemaphoreType.DMA((2,)),
                        pltpu.VMEM((d, f), BF16), pltpu.VMEM((d, f), BF16), pltpu.VMEM((f, d), BF16)],
    )
    return pl.pallas_call(
        _expert_kernel,
        out_shape=jax.ShapeDtypeStruct((n_tiles * MOE_TILE, d), F32),
        grid_spec=grid_spec,
        compiler_params=_params("arbitrary", vmem=VMEM_LIMIT_BYTES),
        name="moe_experts",
    )(tile_expert, n_used, row_src, h, w1, w3, w2)


def _combine_kernel(pos_ref, x_ref, rt_ref, g_ref, ys_hbm, o_ref, ybuf, sem, *, tok0, n_tok):
    i = pl.program_id(0)
    tm = x_ref.shape[0]

    def gather(tile, slot):
        def body(r, carry):
            for s in range(2):
                row = pos_ref[s * n_tok + tok0 + tile * tm + r]
                pltpu.make_async_copy(ys_hbm.at[pl.ds(row, 1)], ybuf.at[slot, s, pl.ds(r, 1)],
                                      sem.at[slot]).start(priority=s)
            return carry
        lax.fori_loop(0, tm, body, 0)

    @pl.when(i == 0)
    def _():
        gather(0, 0)

    @pl.when(i + 1 < pl.num_programs(0))
    def _():
        gather(i + 1, (i + 1) % 2)

    slot = i % 2
    for s in range(2):
        pltpu.make_async_copy(ys_hbm.at[pl.ds(0, tm)], ybuf.at[slot, s], sem.at[slot]).wait()
    rt = rt_ref[...]
    y = x_ref[...] + rt[:, _RT_C1:_RT_C1 + 1] * ybuf[slot, 0] + rt[:, _RT_C2:_RT_C2 + 1] * ybuf[slot, 1]
    o_ref[...] = y * lax.rsqrt(jnp.mean(y * y, axis=-1, keepdims=True) + RMS_EPS) * g_ref[...]


def _combine(x_all, rt, g, ys, pos, *, tok0, count, tm):
    n_tok, d = x_all.shape
    b0 = tok0 // tm
    grid_spec = pltpu.PrefetchScalarGridSpec(
        num_scalar_prefetch=1,
        grid=(count // tm,),
        in_specs=[pl.BlockSpec((tm, d), lambda i, pos: (b0 + i, 0)),
                  pl.BlockSpec((tm, LANES), lambda i, pos: (b0 + i, 0)),
                  pl.BlockSpec((1, d), lambda i, pos: (0, 0)),
                  pl.BlockSpec(memory_space=pl.ANY)],
        out_specs=pl.BlockSpec((tm, d), lambda i, pos: (i, 0)),
        scratch_shapes=[pltpu.VMEM((2, 2, tm, d), F32), pltpu.SemaphoreType.DMA((2,))],
    )
    return pl.pallas_call(
        functools.partial(_combine_kernel, tok0=tok0, n_tok=n_tok),
        out_shape=jax.ShapeDtypeStruct((count, d), F32),
        grid_spec=grid_spec,
        compiler_params=_params("arbitrary"),
        name="moe_combine",
    )(pos.reshape(-1), x_all, rt, g.reshape(1, d), ys)


def kernel(x_prompt, x_sample, mem_prompt, cache_mem_k, cache_mem_v, state_hgrn, state_rwkv, state_rwkv_shift, g_mix, w_in, hg_lb_logits, hg_norm_g, rw_mu, rw_w0, rw_w2, rw_a0, rw_a2, rw_g2, rw_kk, rw_ka, rw_rk, rw_ln_g, rw_ln_b, w_br_hg, w_br_rw, w_out, g_ca, g_mem, w_ca_q, w_ca_k, w_ca_v, w_ca_o, g_moe, w_rg, b_rg, w_re, b_re, w_e1, w_e3, w_e2, g_final):
    bp, seq, d = x_prompt.shape
    ns = x_sample.shape[0]
    tp = bp * seq
    t_all = tp + ns
    assert w_in.shape[0] == 1, "the final RMSNorm is fused into the last layer's MoE combine"
    layer = 0
    hg_w = HG_HEADS * HG_DIM
    rw_w = RW_HEADS * RW_DIM
    shift_w = state_rwkv_shift.shape[-1]
    rw_col = 4 * hg_w
    gate_col = rw_col + shift_w
    mem_t = mem_prompt.shape[1]
    tm = 640
    assert t_all % tm == 0

    lb = jnp.cumsum(jax.nn.softmax(hg_lb_logits.astype(F32), axis=0), axis=0)[layer]
    x_all = jnp.concatenate([x_prompt.reshape(tp, d), x_sample.reshape(ns, d)])

    h = _rmsnorm(x_all, g_mix[layer], BF16, tm)
    p_all = _matmul(h, w_in[layer], tm=tm, tn=1280, out_dtype=F32)
    gn = hg_norm_g[layer]
    o_hg_p, hg_p = _hgrn_prompt(p_all, lb, gn, batch=bp, seq=seq, col0=0, lblk=512)
    o_hg_s, hg_s = _hgrn_step(p_all, state_hgrn.reshape(state_hgrn.shape[1:]), lb, gn, row0=tp, col0=0)
    P = {'rw_w0': rw_w0[layer], 'rw_a0': rw_a0[layer], 'rw_kk': rw_kk[layer], 'rw_ka': rw_ka[layer],
         'rw_rk': rw_rk[layer], 'rw_ln_g': rw_ln_g[layer], 'rw_ln_b': rw_ln_b[layer],
         'rw_w2': rw_w2[layer], 'rw_a2': rw_a2[layer], 'rw_g2': rw_g2[layer]}
    vec, lora = _rw_param_blocks(P)
    mu = rw_mu[layer]
    o_rw_p, rw_p = _rwkv_prompt(p_all, jnp.zeros((bp, shift_w), F32), mu, vec, lora,
                                batch=bp, seq=seq, col0=rw_col // LANES, lblk=512)
    r_s, k_s, v_s, kk_s, kka_s, w_s, g_s, bonus_s = _rwkv_step_prep(
        p_all, state_rwkv_shift.reshape(ns, shift_w), mu, vec, lora, row0=tp, col0=rw_col // LANES)
    y_s, rw_s = _rwkv_step(state_rwkv.reshape(state_rwkv.shape[1:]), r_s, k_s, kk_s, kka_s, w_s, v_s)
    o_rw_s = _rwkv_step_post(y_s, bonus_s, g_s, vec)
    shift_p = p_all[seq - 1:tp:seq, rw_col:gate_col]
    shift_s = p_all[tp:, rw_col:gate_col]
    merged = _merge(jnp.concatenate([o_hg_p, o_hg_s]), jnp.concatenate([o_rw_p, o_rw_s]),
                    w_br_hg[layer], w_br_rw[layer], p_all, gate_col0=gate_col, tm=tm, tn=256)
    x1 = _matmul(merged, w_out[layer], tm=tm, tn=512, out_dtype=F32, res=x_all)

    q = _matmul(_rmsnorm(x1, g_ca[layer], BF16, tm), w_ca_q[layer], tm=tm, tn=512, out_dtype=BF16)
    m = _rmsnorm(mem_prompt.reshape(bp * mem_t, d), g_mem[layer], BF16, mem_t)
    mem_k = _matmul(m, w_ca_k[layer], tm=bp * mem_t, tn=512, out_dtype=F32)
    mem_v = _matmul(m, w_ca_v[layer], tm=bp * mem_t, tn=512, out_dtype=F32)
    o_p = _attn_prompt(q, mem_k.reshape(bp, mem_t, d), mem_v.reshape(bp, mem_t, d), batch=bp, seq=seq, tq=512)
    o_s = _attn_step(q[tp:], cache_mem_k.reshape(ns, mem_t, d), cache_mem_v.reshape(ns, mem_t, d))
    x2 = _matmul(jnp.concatenate([o_p, o_s]), w_ca_o[layer], tm=tm, tn=512, out_dtype=F32, res=x1)

    h3, rt = _router(x2, g_moe[layer], w_rg[layer], b_rg[layer], w_re[layer], b_re[layer], tm=tm)
    n_tiles = (2 * t_all + MOE_EXPERTS * (MOE_TILE - 1)) // MOE_TILE
    row_src, tile_expert, n_used, pos = _dispatch_plan(
        rt[:, _RT_E1].astype(jnp.int32), rt[:, _RT_E2].astype(jnp.int32), n_tiles)
    ys = _experts(h3, w_e1[layer], w_e3[layer], w_e2[layer], row_src, tile_expert, n_used)
    y_p = _combine(x2, rt, g_final, ys, pos, tok0=0, count=tp, tm=LANES)
    y_s = _combine(x2, rt, g_final, ys, pos, tok0=tp, count=ns, tm=LANES)

    ca_heads_shape = (1, bp, mem_t, CA_HEADS, d // CA_HEADS)
    return (y_p.reshape(bp, seq, d), y_s.reshape(ns, 1, d), hg_p[None], rw_p[None], shift_p[None],
            mem_k.reshape(ca_heads_shape), mem_v.reshape(ca_heads_shape), hg_s[None], rw_s[None], shift_s[None])
```
